```python
import jax, jax.numpy as jnp
from jax import lax
import numpy as np

D_MODEL = 4096
BATCH = 8
SEQ = 2048
DEPTH = 2

GRID_W = 64
EPS = 1e-6
D_FF = 11008
MIX_WIDTH = 2 * D_MODEL
D_FOURIER = MIX_WIDTH // 4
FOURIER_GROUPS = 8
FOURIER_GROUP_DIM = D_FOURIER // FOURIER_GROUPS
D_SSD = MIX_WIDTH - D_FOURIER
SSD_HEAD_DIM = 64
SSD_HEADS = D_SSD // SSD_HEAD_DIM
SSD_GROUPS = 8
SSD_HEADS_PER_GROUP = SSD_HEADS // SSD_GROUPS
D_STATE = 128
D_CONV = 5
SSD_CHUNK = 128
CONV_CH = D_SSD + 2 * SSD_GROUPS * D_STATE
IN_COLS = D_FOURIER + D_SSD + CONV_CH + 2 * SSD_HEADS
N_HEADS = 32
N_KV_HEADS = 8
KV_GROUP = N_HEADS // N_KV_HEADS
HEAD_DIM = D_MODEL // N_HEADS
ROPE_HALF = HEAD_DIM // 2
ROPE_THETA = 10000.0
Q_BLOCK = 128
QKV_COLS = (N_HEADS + 2 * N_KV_HEADS) * HEAD_DIM
N_EVEN = (DEPTH + 1) // 2
N_ODD = DEPTH // 2

kernel_name = "hybrid_fnet_ssd_axial_gqa_macaron"


def rmsnorm(x, w):
    xf = x.astype(jnp.float32)
    y = xf * lax.rsqrt(jnp.mean(xf * xf, axis=-1, keepdims=True) + EPS)
    return (y * w.astype(jnp.float32)).astype(x.dtype)


def swiglu_ffn(h, w_gate, w_up, w_down):
    return (jax.nn.silu(h @ w_gate) * (h @ w_up)) @ w_down


def fourier_mix(u):
    b, l, _ = u.shape
    uf = u.astype(jnp.float32).reshape(b, l, FOURIER_GROUPS, FOURIER_GROUP_DIM)
    y = jnp.fft.fft2(uf, axes=(1, 3), norm="ortho").real
    return y.reshape(b, l, D_FOURIER).astype(u.dtype)


def depthwise_conv_centred(x, w, bias):
    out = lax.conv_general_dilated(
        x, w[:, None, :].astype(x.dtype), window_strides=(1,),
        padding=[((D_CONV - 1) // 2, D_CONV // 2)],
        dimension_numbers=("NWC", "WIO", "NWC"),
        feature_group_count=x.shape[-1])
    return out + bias.astype(x.dtype)


def ssd_chunked(x, dt, A, B, C):
    b, l, g, k, p = x.shape
    n = B.shape[-1]
    q = SSD_CHUNK
    c = l // q
    xs = (x * dt[..., None]).reshape(b, c, q, g, k, p)
    a_cum = jnp.cumsum((dt * A).reshape(b, c, q, g, k), axis=2)
    Bc = B.reshape(b, c, q, g, n)
    Cc = C.reshape(b, c, q, g, n)
    lower = jnp.tril(jnp.ones((q, q), dtype=bool))[:, :, None, None]
    seg = a_cum[:, :, :, None] - a_cum[:, :, None, :]
    decay = jnp.exp(jnp.where(lower, seg, -jnp.inf))
    cb = jnp.einsum("bctgn,bcsgn->bctsg", Cc, Bc)
    y_diag = jnp.einsum("bctsg,bctsgk,bcsgkp->bctgkp", cb, decay, xs)
    decay_to_end = jnp.exp(a_cum[:, :, -1:] - a_cum)
    states = jnp.einsum("bcsgn,bcsgk,bcsgkp->bcgkpn", Bc, decay_to_end, xs)
    chunk_decay = jnp.exp(a_cum[:, :, -1])

    def step(h, inp):
        s, d = inp
        return h * d[..., None, None] + s, h

    h0 = jnp.zeros((b, g, k, p, n), xs.dtype)
    _, prev = lax.scan(step, h0, (jnp.moveaxis(states, 1, 0), jnp.moveaxis(chunk_decay, 1, 0)))
    prev = jnp.moveaxis(prev, 0, 1)
    y_off = jnp.einsum("bctgn,bcgkpn,bctgk->bctgkp", Cc, prev, jnp.exp(a_cum))
    return (y_diag + y_off).reshape(b, l, g, k, p)


def ssd_mixer(z, xbc, dt_raw, conv_w, conv_b, A_log, dt_bias, D_skip, gnorm_w):
    b, l, _ = z.shape
    G, K, P, N = SSD_GROUPS, SSD_HEADS_PER_GROUP, SSD_HEAD_DIM, D_STATE
    xbc = jax.nn.silu(depthwise_conv_centred(xbc, conv_w, conv_b)).astype(jnp.float32)
    xs = xbc[..., :D_SSD].reshape(b, l, G, K, P)
    Bm = xbc[..., D_SSD:D_SSD + G * N].reshape(b, l, G, N)
    Cm = xbc[..., D_SSD + G * N:].reshape(b, l, G, N)
    dt = jax.nn.softplus(dt_raw.astype(jnp.float32).reshape(b, l, 2, G, K)
                         + dt_bias.astype(jnp.float32).reshape(2, G, K))
    A = -jnp.exp(A_log.astype(jnp.float32)).reshape(2, G, K)
    flip = lambda t: jnp.flip(t, axis=1)
    y_fwd = ssd_chunked(xs, dt[:, :, 0], A[0], Bm, Cm)
    y_bwd = flip(ssd_chunked(flip(xs), flip(dt[:, :, 1]), A[1], flip(Bm), flip(Cm)))
    y = y_fwd + y_bwd + D_skip.astype(jnp.float32).reshape(G, K)[..., None] * xs
    y = y.reshape(b, l, D_SSD) * jax.nn.silu(z.astype(jnp.float32))
    yg = y.reshape(b, l, G, D_SSD // G)
    yg = yg * lax.rsqrt(jnp.mean(yg * yg, axis=-1, keepdims=True) + EPS)
    y = yg.reshape(b, l, D_SSD) * gnorm_w.astype(jnp.float32)
    return y.astype(z.dtype)


def fourier_ssd_layer(h, in_proj, conv_w, conv_b, A_log, dt_bias, D_skip, gnorm_w, out_proj):
    proj = h @ in_proj
    o1 = D_FOURIER
    o2 = o1 + D_SSD
    o3 = o2 + CONV_CH
    y_f = fourier_mix(proj[..., :o1])
    y_s = ssd_mixer(proj[..., o1:o2], proj[..., o2:o3], proj[..., o3:],
                    conv_w, conv_b, A_log, dt_bias, D_skip, gnorm_w)
    return jnp.concatenate([y_f, y_s], axis=-1) @ out_proj


def axial_rope_tables(l):
    rows = l // GRID_W
    row_idx = jnp.repeat(jnp.arange(rows), GRID_W)
    col_idx = jnp.tile(jnp.arange(GRID_W), rows)
    inv_freq = ROPE_THETA ** (-jnp.arange(0, ROPE_HALF, 2, dtype=jnp.float32) / ROPE_HALF)
    ang = jnp.stack([row_idx, col_idx], 0).astype(jnp.float32)[..., None] * inv_freq
    ang = jnp.moveaxis(ang, 0, 1)[:, None]
    return jnp.cos(ang), jnp.sin(ang)


def apply_axial_rope(x, cos, sin):
    xr = x.reshape(*x.shape[:-1], 2, 2, ROPE_HALF // 2)
    x1 = xr[..., 0, :]
    x2 = xr[..., 1, :]
    out = jnp.stack([x1 * cos - x2 * sin, x2 * cos + x1 * sin], axis=-2)
    return out.reshape(x.shape)


def gqa_axial_attention(h, w_qkv, q_norm, k_norm, w_o):
    b, l, _ = h.shape
    qkv = h @ w_qkv
    nq = N_HEADS * HEAD_DIM
    nk = N_KV_HEADS * HEAD_DIM
    q = rmsnorm(qkv[..., :nq].reshape(b, l, N_HEADS, HEAD_DIM), q_norm).astype(jnp.float32)
    k = rmsnorm(qkv[..., nq:nq + nk].reshape(b, l, N_KV_HEADS, HEAD_DIM), k_norm).astype(jnp.float32)
    v = qkv[..., nq + nk:].reshape(b, l, N_KV_HEADS, HEAD_DIM)
    cos, sin = axial_rope_tables(l)
    q = (apply_axial_rope(q, cos, sin) * HEAD_DIM ** -0.5).astype(h.dtype)
    k = apply_axial_rope(k, cos, sin).astype(h.dtype)
    q = q.reshape(b, l // Q_BLOCK, Q_BLOCK, N_KV_HEADS, KV_GROUP, HEAD_DIM)
    q = jnp.moveaxis(q, 1, 0)

    def attend_block(qb):
        s = jnp.einsum("bqkgd,bskd->bkgqs", qb, k).astype(jnp.float32)
        p = jax.nn.softmax(s, axis=-1)
        return jnp.einsum("bkgqs,bskd->bqkgd", p.astype(v.dtype), v)

    o = lax.map(attend_block, q)
    o = jnp.moveaxis(o, 0, 1).reshape(b, l, N_HEADS * HEAD_DIM)
    return o @ w_o


def setup_inputs(seed: int = 0) -> dict:
    key = jax.random.key(seed)
    ks = jax.random.split(key, 20)
    nrm = jax.random.normal
    dt = jnp.exp(jax.random.uniform(ks[10], (N_EVEN, 2, SSD_HEADS))
                 * (np.log(0.1) - np.log(0.001)) + np.log(0.001))
    return {
        "x": nrm(ks[0], (BATCH, SEQ, D_MODEL)),
        "ffn_norm": 1.0 + 0.02 * nrm(ks[1], (DEPTH, 2, D_MODEL)),
        "ffn_w_gate": nrm(ks[2], (DEPTH, 2, D_MODEL, D_FF)) * D_MODEL ** -0.5,
        "ffn_w_up": nrm(ks[3], (DEPTH, 2, D_MODEL, D_FF)) * D_MODEL ** -0.5,
        "ffn_w_down": nrm(ks[4], (DEPTH, 2, D_FF, D_MODEL)) * D_FF ** -0.5,
        "mix_norm": 1.0 + 0.02 * nrm(ks[5], (DEPTH, D_MODEL)),
        "hyb_in_proj": nrm(ks[6], (N_EVEN, D_MODEL, IN_COLS)) * D_MODEL ** -0.5,
        "ssd_conv_w": nrm(ks[7], (N_EVEN, D_CONV, CONV_CH)) * D_CONV ** -0.5,
        "ssd_conv_b": 0.02 * nrm(ks[8], (N_EVEN, CONV_CH)),
        "ssd_A_log": jnp.log(jax.random.uniform(ks[9], (N_EVEN, 2, SSD_HEADS), minval=1.0, maxval=16.0)),
        "ssd_dt_bias": dt + jnp.log(-jnp.expm1(-dt)),
        "ssd_D": 1.0 + 0.02 * nrm(ks[11], (N_EVEN, SSD_HEADS)),
        "ssd_gnorm": 1.0 + 0.02 * nrm(ks[12], (N_EVEN, D_SSD)),
        "hyb_out_proj": nrm(ks[13], (N_EVEN, MIX_WIDTH, D_MODEL)) * MIX_WIDTH ** -0.5,
        "attn_w_qkv": nrm(ks[14], (N_ODD, D_MODEL, QKV_COLS)) * D_MODEL ** -0.5,
        "attn_q_norm": 1.0 + 0.02 * nrm(ks[15], (N_ODD, HEAD_DIM)),
        "attn_k_norm": 1.0 + 0.02 * nrm(ks[16], (N_ODD, HEAD_DIM)),
        "attn_w_o": nrm(ks[17], (N_ODD, N_HEADS * HEAD_DIM, D_MODEL)) * (N_HEADS * HEAD_DIM) ** -0.5,
        "final_norm": 1.0 + 0.02 * nrm(ks[18], (D_MODEL,)),
    }


def reference(x, ffn_norm, ffn_w_gate, ffn_w_up, ffn_w_down, mix_norm, hyb_in_proj,
              ssd_conv_w, ssd_conv_b, ssd_A_log, ssd_dt_bias, ssd_D, ssd_gnorm,
              hyb_out_proj, attn_w_qkv, attn_q_norm, attn_k_norm, attn_w_o, final_norm):
    for i in range(DEPTH):
        j = i // 2
        x = x + 0.5 * swiglu_ffn(rmsnorm(x, ffn_norm[i, 0]),
                                 ffn_w_gate[i, 0], ffn_w_up[i, 0], ffn_w_down[i, 0])
        hn = rmsnorm(x, mix_norm[i])
        if i % 2 == 0:
            x = x + fourier_ssd_layer(hn, hyb_in_proj[j], ssd_conv_w[j], ssd_conv_b[j],
                                      ssd_A_log[j], ssd_dt_bias[j], ssd_D[j], ssd_gnorm[j],
                                      hyb_out_proj[j])
        else:
            x = x + gqa_axial_attention(hn, attn_w_qkv[j], attn_q_norm[j], attn_k_norm[j], attn_w_o[j])
        x = x + 0.5 * swiglu_ffn(rmsnorm(x, ffn_norm[i, 1]),
                                 ffn_w_gate[i, 1], ffn_w_up[i, 1], ffn_w_down[i, 1])
    return rmsnorm(x, final_norm)
```

```python
import functools

import jax
import jax.numpy as jnp
from jax import lax
from jax.experimental import pallas as pl
from jax.experimental.pallas import tpu as pltpu

D_MODEL = 4096
SEQ = 2048
GRID_W = 64
EPS = 1e-6
D_FF = 11008
MIX_WIDTH = 2 * D_MODEL
D_FOURIER = MIX_WIDTH // 4
FOURIER_GROUPS = 8
FOURIER_GROUP_DIM = D_FOURIER // FOURIER_GROUPS
D_SSD = MIX_WIDTH - D_FOURIER
SSD_HEAD_DIM = 64
SSD_HEADS = D_SSD // SSD_HEAD_DIM
SSD_GROUPS = 8
SSD_HEADS_PER_GROUP = SSD_HEADS // SSD_GROUPS
D_STATE = 128
D_CONV = 5
SSD_CHUNK = 128
CONV_CH = D_SSD + 2 * SSD_GROUPS * D_STATE
N_HEADS = 32
N_KV_HEADS = 8
KV_GROUP = N_HEADS // N_KV_HEADS
HEAD_DIM = D_MODEL // N_HEADS
ROPE_HALF = HEAD_DIM // 2
ROPE_THETA = 10000.0
Q_BLOCK = 128

V7X_VMEM_LIMIT_BYTES = 56 * 1024 * 1024


def _params(*semantics):
    return pltpu.CompilerParams(dimension_semantics=semantics,
                                vmem_limit_bytes=V7X_VMEM_LIMIT_BYTES)


def _rmsnorm_kernel(x_ref, w_ref, o_ref):
    x = x_ref[...]
    ms = jnp.mean(x * x, axis=-1, keepdims=True)
    o_ref[...] = (x * lax.rsqrt(ms + EPS) * w_ref[...]).astype(o_ref.dtype)


def _rmsnorm(x, w, out_dtype=jnp.bfloat16, *, tm=256):
    t, d = x.shape
    return pl.pallas_call(
        _rmsnorm_kernel,
        out_shape=jax.ShapeDtypeStruct((t, d), out_dtype),
        grid=(t // tm,),
        in_specs=[pl.BlockSpec((tm, d), lambda i: (i, 0)),
                  pl.BlockSpec((1, d), lambda i: (0, 0))],
        out_specs=pl.BlockSpec((tm, d), lambda i: (i, 0)),
        compiler_params=_params("arbitrary"),
        name="rmsnorm",
    )(x, w.reshape(1, d))


FFN_DOWN_CHUNK = 512


def _ffn_kernel(hn_ref, x_ref, wg_ref, wu_ref, wd_ref, o_ref):
    j = pl.program_id(1)

    @pl.when(j == 0)
    def _():
        o_ref[...] = x_ref[...]

    hn = hn_ref[...]
    g = jnp.dot(hn, wg_ref[...], preferred_element_type=jnp.float32)
    u = jnp.dot(hn, wu_ref[...], preferred_element_type=jnp.float32)
    a = (0.5 * g / (1.0 + jnp.exp(-g)) * u).astype(jnp.bfloat16)
    for n in range(0, o_ref.shape[1], FFN_DOWN_CHUNK):
        cols = slice(n, n + FFN_DOWN_CHUNK)
        o_ref[:, cols] += jnp.dot(a, wd_ref[:, cols], preferred_element_type=jnp.float32)


def _ffn(hn, x, wg, wu, wd, *, tm=512, tf=256):
    t, d = x.shape
    f = wg.shape[1]
    return pl.pallas_call(
        _ffn_kernel,
        out_shape=jax.ShapeDtypeStruct((t, d), jnp.float32),
        grid=(t // tm, f // tf),
        in_specs=[pl.BlockSpec((tm, d), lambda i, j: (i, 0)),
                  pl.BlockSpec((tm, d), lambda i, j: (i, 0)),
                  pl.BlockSpec((d, tf), lambda i, j: (0, j)),
                  pl.BlockSpec((d, tf), lambda i, j: (0, j)),
                  pl.BlockSpec((tf, d), lambda i, j: (j, 0))],
        out_specs=pl.BlockSpec((tm, d), lambda i, j: (i, 0)),
        compiler_params=_params("arbitrary", "arbitrary"),
        name="ffn",
    )(hn, x, wg, wu, wd)


def _mm_kernel(a_ref, w_ref, o_ref):
    o_ref[...] = jnp.dot(a_ref[...], w_ref[...],
                         preferred_element_type=jnp.float32).astype(o_ref.dtype)


def _matmul(a, w, out_dtype, *, tm=1024, tn=1024):
    t, k = a.shape
    n = w.shape[1]
    return pl.pallas_call(
        _mm_kernel,
        out_shape=jax.ShapeDtypeStruct((t, n), out_dtype),
        grid=(t // tm, n // tn),
        in_specs=[pl.BlockSpec((tm, k), lambda i, j: (i, 0)),
                  pl.BlockSpec((k, tn), lambda i, j: (0, j))],
        out_specs=pl.BlockSpec((tm, tn), lambda i, j: (i, j)),
        compiler_params=_params("arbitrary", "arbitrary"),
        name="matmul",
    )(a, w)


def _mm_res_kernel(a_ref, w_ref, x_ref, o_ref):
    k = pl.program_id(2)

    @pl.when(k == 0)
    def _():
        o_ref[...] = x_ref[...]

    o_ref[...] += jnp.dot(a_ref[...], w_ref[...], preferred_element_type=jnp.float32)


def _matmul_res(a, w, x, *, tm=1024, tn=1024, tk=4096):
    t, kk = a.shape
    n = w.shape[1]
    return pl.pallas_call(
        _mm_res_kernel,
        out_shape=jax.ShapeDtypeStruct((t, n), jnp.float32),
        grid=(t // tm, n // tn, kk // tk),
        in_specs=[pl.BlockSpec((tm, tk), lambda i, j, k: (i, k)),
                  pl.BlockSpec((tk, tn), lambda i, j, k: (k, j)),
                  pl.BlockSpec((tm, tn), lambda i, j, k: (i, j))],
        out_specs=pl.BlockSpec((tm, tn), lambda i, j, k: (i, j)),
        compiler_params=_params("arbitrary", "arbitrary", "arbitrary"),
        name="matmul_res",
    )(a, w, x)


def _fourier_mix(u):
    b, l, _ = u.shape
    uf = u.reshape(b, l, FOURIER_GROUPS, FOURIER_GROUP_DIM)
    y = jnp.fft.fft2(uf, axes=(1, 3), norm="ortho").real
    return y.reshape(b, l, D_FOURIER)


def _depthwise_conv_centred(x, w, bias):
    out = lax.conv_general_dilated(
        x, w[:, None, :], window_strides=(1,),
        padding=[((D_CONV - 1) // 2, D_CONV // 2)],
        dimension_numbers=("NWC", "WIO", "NWC"),
        feature_group_count=x.shape[-1])
    return out + bias


def _ssd_chunked(x, dt, A, B, C):
    b, l, g, k, p = x.shape
    n = B.shape[-1]
    q = SSD_CHUNK
    c = l // q
    xs = (x * dt[..., None]).reshape(b, c, q, g, k, p)
    a_cum = jnp.cumsum((dt * A).reshape(b, c, q, g, k), axis=2)
    Bc = B.reshape(b, c, q, g, n)
    Cc = C.reshape(b, c, q, g, n)
    lower = jnp.tril(jnp.ones((q, q), dtype=bool))[:, :, None, None]
    seg = a_cum[:, :, :, None] - a_cum[:, :, None, :]
    decay = jnp.exp(jnp.where(lower, seg, -jnp.inf))
    cb = jnp.einsum("bctgn,bcsgn->bctsg", Cc, Bc)
    y_diag = jnp.einsum("bctsg,bctsgk,bcsgkp->bctgkp", cb, decay, xs)
    decay_to_end = jnp.exp(a_cum[:, :, -1:] - a_cum)
    states = jnp.einsum("bcsgn,bcsgk,bcsgkp->bcgkpn", Bc, decay_to_end, xs)
    chunk_decay = jnp.exp(a_cum[:, :, -1])

    def step(h, inp):
        s, d = inp
        return h * d[..., None, None] + s, h

    h0 = jnp.zeros((b, g, k, p, n), xs.dtype)
    _, prev = lax.scan(step, h0, (jnp.moveaxis(states, 1, 0), jnp.moveaxis(chunk_decay, 1, 0)))
    prev = jnp.moveaxis(prev, 0, 1)
    y_off = jnp.einsum("bctgn,bcgkpn,bctgk->bctgkp", Cc, prev, jnp.exp(a_cum))
    return (y_diag + y_off).reshape(b, l, g, k, p)


def _ssd_mixer(z, xbc, dt_raw, conv_w, conv_b, A_log, dt_bias, D_skip, gnorm_w):
    b, l, _ = z.shape
    G, K, P, N = SSD_GROUPS, SSD_HEADS_PER_GROUP, SSD_HEAD_DIM, D_STATE
    xbc = jax.nn.silu(_depthwise_conv_centred(xbc, conv_w, conv_b))
    xs = xbc[..., :D_SSD].reshape(b, l, G, K, P)
    Bm = xbc[..., D_SSD:D_SSD + G * N].reshape(b, l, G, N)
    Cm = xbc[..., D_SSD + G * N:].reshape(b, l, G, N)
    dt = jax.nn.softplus(dt_raw.reshape(b, l, 2, G, K) + dt_bias.reshape(2, G, K))
    A = -jnp.exp(A_log).reshape(2, G, K)
    flip = lambda t: jnp.flip(t, axis=1)
    y_fwd = _ssd_chunked(xs, dt[:, :, 0], A[0], Bm, Cm)
    y_bwd = flip(_ssd_chunked(flip(xs), flip(dt[:, :, 1]), A[1], flip(Bm), flip(Cm)))
    y = y_fwd + y_bwd + D_skip.reshape(G, K)[..., None] * xs
    y = y.reshape(b, l, D_SSD) * jax.nn.silu(z)
    yg = y.reshape(b, l, G, D_SSD // G)
    yg = yg * lax.rsqrt(jnp.mean(yg * yg, axis=-1, keepdims=True) + EPS)
    return yg.reshape(b, l, D_SSD) * gnorm_w


def _axial_rope_tables(l):
    rows = l // GRID_W
    row_idx = jnp.repeat(jnp.arange(rows), GRID_W)
    col_idx = jnp.tile(jnp.arange(GRID_W), rows)
    inv_freq = ROPE_THETA ** (-jnp.arange(0, ROPE_HALF, 2, dtype=jnp.float32) / ROPE_HALF)
    ang = jnp.stack([row_idx, col_idx], 0).astype(jnp.float32)[..., None] * inv_freq
    ang = jnp.moveaxis(ang, 0, 1)[:, None]
    return jnp.cos(ang), jnp.sin(ang)


def _apply_axial_rope(x, cos, sin):
    xr = x.reshape(*x.shape[:-1], 2, 2, ROPE_HALF // 2)
    x1 = xr[..., 0, :]
    x2 = xr[..., 1, :]
    out = jnp.stack([x1 * cos - x2 * sin, x2 * cos + x1 * sin], axis=-2)
    return out.reshape(x.shape)


def _head_rmsnorm(x, w):
    return x * lax.rsqrt(jnp.mean(x * x, axis=-1, keepdims=True) + EPS) * w


def _attention_core(qkv, q_norm, k_norm, b, l):
    nq = N_HEADS * HEAD_DIM
    nk = N_KV_HEADS * HEAD_DIM
    q = _head_rmsnorm(qkv[..., :nq].reshape(b, l, N_HEADS, HEAD_DIM), q_norm)
    k = _head_rmsnorm(qkv[..., nq:nq + nk].reshape(b, l, N_KV_HEADS, HEAD_DIM), k_norm)
    v = qkv[..., nq + nk:].reshape(b, l, N_KV_HEADS, HEAD_DIM)
    cos, sin = _axial_rope_tables(l)
    q = _apply_axial_rope(q, cos, sin) * HEAD_DIM ** -0.5
    k = _apply_axial_rope(k, cos, sin)
    q = q.reshape(b, l // Q_BLOCK, Q_BLOCK, N_KV_HEADS, KV_GROUP, HEAD_DIM)
    q = jnp.moveaxis(q, 1, 0)

    def attend_block(qb):
        s = jnp.einsum("bqkgd,bskd->bkgqs", qb, k)
        p = jax.nn.softmax(s, axis=-1)
        return jnp.einsum("bkgqs,bskd->bqkgd", p, v)

    o = lax.map(attend_block, q)
    return jnp.moveaxis(o, 0, 1).reshape(b, l, N_HEADS * HEAD_DIM)


def kernel(x, ffn_norm, ffn_w_gate, ffn_w_up, ffn_w_down, mix_norm, hyb_in_proj,
           ssd_conv_w, ssd_conv_b, ssd_A_log, ssd_dt_bias, ssd_D, ssd_gnorm,
           hyb_out_proj, attn_w_qkv, attn_q_norm, attn_k_norm, attn_w_o, final_norm):
    b, l, d = x.shape
    t = b * l
    bf = jnp.bfloat16
    x = x.reshape(t, d)

    def ffn_half(x, i, h):
        hn = _rmsnorm(x, ffn_norm[i, h])
        return _ffn(hn, x, ffn_w_gate[i, h].astype(bf), ffn_w_up[i, h].astype(bf),
                    ffn_w_down[i, h].astype(bf))

    x = ffn_half(x, 0, 0)
    hn = _rmsnorm(x, mix_norm[0])
    o3 = D_FOURIER + D_SSD + CONV_CH
    w_in = hyb_in_proj[0]
    proj = _matmul(hn, w_in[:, :o3].astype(bf), jnp.float32)
    dt_raw = _matmul(hn, jnp.pad(w_in[:, o3:], ((0, 0), (0, 256 - 2 * SSD_HEADS))).astype(bf),
                     jnp.float32, tn=256)[:, :2 * SSD_HEADS]
    proj = proj.reshape(b, l, o3)
    o1 = D_FOURIER
    o2 = o1 + D_SSD
    y_f = _fourier_mix(proj[..., :o1])
    y_s = _ssd_mixer(proj[..., o1:o2], proj[..., o2:], dt_raw.reshape(b, l, -1),
                     ssd_conv_w[0], ssd_conv_b[0], ssd_A_log[0], ssd_dt_bias[0],
                     ssd_D[0], ssd_gnorm[0])
    y = jnp.concatenate([y_f, y_s], axis=-1).reshape(t, MIX_WIDTH).astype(bf)
    x = _matmul_res(y, hyb_out_proj[0].astype(bf), x)
    x = ffn_half(x, 0, 1)

    x = ffn_half(x, 1, 0)
    hn = _rmsnorm(x, mix_norm[1])
    qkv = _matmul(hn, attn_w_qkv[0].astype(bf), jnp.float32)
    o = _attention_core(qkv.reshape(b, l, -1), attn_q_norm[0], attn_k_norm[0], b, l)
    x = _matmul_res(o.reshape(t, -1).astype(bf), attn_w_o[0].astype(bf), x)
    x = ffn_half(x, 1, 1)

    return _rmsnorm(x, final_norm, jnp.float32).reshape(b, l, d)
```

```python
import functools
import math

import jax
import jax.numpy as jnp
from jax import lax
from jax.experimental import pallas as pl
from jax.experimental.pallas import tpu as pltpu

EPS = 1e-6
GRID_W = 64
FOURIER_GROUPS = 8
SSD_HEAD_DIM = 64
SSD_GROUPS = 8
D_STATE = 128
D_CONV = 5
SSD_CHUNK = 128
N_HEADS = 32
N_KV_HEADS = 8
KV_GROUP = N_HEADS // N_KV_HEADS
HEAD_DIM = 128
ROPE_HALF = HEAD_DIM // 2
ROPE_THETA = 10000.0

LANES = 128
SUBLANES = 8
V7X_VMEM_LIMIT_BYTES = 56 * 1024 * 1024

BF16 = jnp.bfloat16
F32 = jnp.float32


def _params(*semantics):
    return pltpu.CompilerParams(dimension_semantics=semantics,
                                vmem_limit_bytes=V7X_VMEM_LIMIT_BYTES)


def _dot(a, b):
    return jnp.dot(a, b, preferred_element_type=F32)


def _dot_nt(a, b):
    return lax.dot_general(a, b, (((1,), (1,)), ((), ())), preferred_element_type=F32)


def _dot_tn(a, b):
    return lax.dot_general(a, b, (((0,), (0,)), ((), ())), preferred_element_type=F32)


def _sigmoid(x):
    return 1.0 / (1.0 + jnp.exp(-x))


def _rmsnorm_kernel(x_ref, w_ref, o_ref):
    x = x_ref[...]
    ms = jnp.mean(x * x, axis=-1, keepdims=True)
    o_ref[...] = (x * lax.rsqrt(ms + EPS) * w_ref[...]).astype(o_ref.dtype)


def _rmsnorm(x, w, out_dtype=BF16, *, tm=256):
    t, d = x.shape
    return pl.pallas_call(
        _rmsnorm_kernel,
        out_shape=jax.ShapeDtypeStruct((t, d), out_dtype),
        grid=(t // tm,),
        in_specs=[pl.BlockSpec((tm, d), lambda i: (i, 0)),
                  pl.BlockSpec((1, d), lambda i: (0, 0))],
        out_specs=pl.BlockSpec((tm, d), lambda i: (i, 0)),
        compiler_params=_params("arbitrary"),
        name="rmsnorm",
    )(x, w.reshape(1, d))


FFN_DOWN_CHUNK = 512


def _ffn_kernel(hn_ref, x_ref, wg_ref, wu_ref, wd_ref, o_ref):
    j = pl.program_id(1)

    @pl.when(j == 0)
    def _():
        o_ref[...] = x_ref[...]

    hn = hn_ref[...]
    g = _dot(hn, wg_ref[...])
    u = _dot(hn, wu_ref[...])
    a = (0.5 * g / (1.0 + jnp.exp(-g)) * u).astype(BF16)
    for n in range(0, o_ref.shape[1], FFN_DOWN_CHUNK):
        cols = slice(n, n + FFN_DOWN_CHUNK)
        o_ref[:, cols] += _dot(a, wd_ref[:, cols])


def _ffn(hn, x, wg, wu, wd, *, tm=512, tf=256):
    t, d = x.shape
    f = wg.shape[1]
    return pl.pallas_call(
        _ffn_kernel,
        out_shape=jax.ShapeDtypeStruct((t, d), F32),
        grid=(t // tm, f // tf),
        in_specs=[pl.BlockSpec((tm, d), lambda i, j: (i, 0)),
                  pl.BlockSpec((tm, d), lambda i, j: (i, 0)),
                  pl.BlockSpec((d, tf), lambda i, j: (0, j)),
                  pl.BlockSpec((d, tf), lambda i, j: (0, j)),
                  pl.BlockSpec((tf, d), lambda i, j: (j, 0))],
        out_specs=pl.BlockSpec((tm, d), lambda i, j: (i, 0)),
        compiler_params=_params("arbitrary", "arbitrary"),
        name="ffn",
    )(hn, x, wg, wu, wd)


def _mm_kernel(a_ref, w_ref, o_ref):
    o_ref[...] = _dot(a_ref[...], w_ref[...]).astype(o_ref.dtype)


def _matmul(a, w, out_dtype, *, tm=1024, tn=1024):
    t, k = a.shape
    n = w.shape[1]
    tm, tn = min(tm, t), min(tn, n)
    return pl.pallas_call(
        _mm_kernel,
        out_shape=jax.ShapeDtypeStruct((t, n), out_dtype),
        grid=(t // tm, n // tn),
        in_specs=[pl.BlockSpec((tm, k), lambda i, j: (i, 0)),
                  pl.BlockSpec((k, tn), lambda i, j: (0, j))],
        out_specs=pl.BlockSpec((tm, tn), lambda i, j: (i, j)),
        compiler_params=_params("arbitrary", "arbitrary"),
        name="matmul",
    )(a, w)


def _mm_res_kernel(a_ref, w_ref, x_ref, o_ref):
    o_ref[...] = x_ref[...] + _dot(a_ref[...], w_ref[...])


def _matmul_res(a, w, x, *, tm=1024, tn=1024):
    t, k = a.shape
    n = w.shape[1]
    tm, tn = min(tm, t), min(tn, n)
    return pl.pallas_call(
        _mm_res_kernel,
        out_shape=jax.ShapeDtypeStruct((t, n), F32),
        grid=(t // tm, n // tn),
        in_specs=[pl.BlockSpec((tm, k), lambda i, j: (i, 0)),
                  pl.BlockSpec((k, tn), lambda i, j: (0, j)),
                  pl.BlockSpec((tm, tn), lambda i, j: (i, j))],
        out_specs=pl.BlockSpec((tm, tn), lambda i, j: (i, j)),
        compiler_params=_params("arbitrary", "arbitrary"),
        name="matmul_res",
    )(a, w, x)


def _mm2_res_kernel(a0_ref, a1_ref, w_ref, x_ref, o_ref):
    k = pl.program_id(2)

    @pl.when(k == 0)
    def _():
        o_ref[...] = x_ref[...] + _dot(a0_ref[...], w_ref[...])

    @pl.when(k > 0)
    def _():
        o_ref[...] += _dot(a1_ref[...], w_ref[...])


def _matmul2_res(a0, a1, w, x, *, tm=1024, tn=1024):
    t, tk = a0.shape
    n = w.shape[1]
    tm, tn = min(tm, t), min(tn, n)
    nk = 1 + a1.shape[1] // tk
    return pl.pallas_call(
        _mm2_res_kernel,
        out_shape=jax.ShapeDtypeStruct((t, n), F32),
        grid=(t // tm, n // tn, nk),
        in_specs=[pl.BlockSpec((tm, tk), lambda i, j, k: (i, 0)),
                  pl.BlockSpec((tm, tk), lambda i, j, k: (i, jnp.maximum(k - 1, 0))),
                  pl.BlockSpec((tk, tn), lambda i, j, k: (k, j)),
                  pl.BlockSpec((tm, tn), lambda i, j, k: (i, j))],
        out_specs=pl.BlockSpec((tm, tn), lambda i, j, k: (i, j)),
        compiler_params=_params("arbitrary", "arbitrary", "arbitrary"),
        name="matmul2_res",
    )(a0, a1, w, x)


def _fourier_kernel(u_ref, wc_ref, dl_ref, o_ref, *, scale):
    c = u_ref.shape[1]
    w1 = _dot(u_ref[...], wc_ref[...]).astype(BF16)
    stacked = jnp.concatenate([w1[:, :c], w1[:, c:]], axis=0)
    o_ref[...] = (_dot(dl_ref[...], stacked) * scale).astype(o_ref.dtype)


def _dft_tables(n):
    idx = jnp.arange(n, dtype=jnp.int32)
    ang = ((idx[:, None] * idx[None, :]) % n).astype(F32) * (2.0 * math.pi / n)
    return jnp.cos(ang), jnp.sin(ang)


def _fourier_mix(u, batch, seqlen):
    t, width = u.shape
    c = width // FOURIER_GROUPS
    cc, sc = _dft_tables(c)
    cl, sl = _dft_tables(seqlen)
    wc = jnp.concatenate([cc, sc], axis=1).astype(BF16)
    dl = jnp.concatenate([cl, -sl], axis=1).astype(BF16)
    return pl.pallas_call(
        functools.partial(_fourier_kernel, scale=1.0 / math.sqrt(seqlen * c)),
        out_shape=jax.ShapeDtypeStruct((t, width), BF16),
        grid=(batch, FOURIER_GROUPS),
        in_specs=[pl.BlockSpec((seqlen, c), lambda b, g: (b, g)),
                  pl.BlockSpec((c, 2 * c), lambda b, g: (0, 0)),
                  pl.BlockSpec((seqlen, 2 * seqlen), lambda b, g: (0, 0))],
        out_specs=pl.BlockSpec((seqlen, c), lambda b, g: (b, g)),
        compiler_params=_params("arbitrary", "arbitrary"),
        name="fourier",
    )(u, wc, dl)


def _split3(x):
    h1 = x.astype(BF16)
    r1 = x - h1.astype(F32)
    h2 = r1.astype(BF16)
    h3 = (r1 - h2.astype(F32)).astype(BF16)
    return h1, h2, h3


def _softplus(v):
    return jnp.maximum(v, 0.0) + jnp.log(1.0 + jnp.exp(-jnp.abs(v)))


def _conv_silu_chunk(raw_ref, w_ref, b_ref, c, n_chunks):
    q = SSD_CHUNK
    seqlen = raw_ref.shape[0]
    r0 = pl.multiple_of(c * q, q)
    mid = raw_ref[pl.ds(r0, q), :]
    top = raw_ref[pl.ds(pl.multiple_of(jnp.maximum(r0 - SUBLANES, 0), SUBLANES), SUBLANES), :]
    bot = raw_ref[pl.ds(pl.multiple_of(jnp.minimum(r0 + q, seqlen - SUBLANES), SUBLANES), SUBLANES), :]
    top = jnp.where(c > 0, top, 0.0)
    bot = jnp.where(c < n_chunks - 1, bot, 0.0)
    win = jnp.concatenate([top, mid, bot], axis=0)
    acc = b_ref[...]
    pad = (D_CONV - 1) // 2
    for j in range(D_CONV):
        lo = SUBLANES - pad + j
        acc = acc + w_ref[j:j + 1, :] * win[lo:lo + q, :]
    return acc * _sigmoid(acc)


def _dt_terms_col(dtc_ref, bias_ref, alog_ref, r0, lo_tri, up_tri, nh):
    q = SSD_CHUNK
    dt = _softplus(dtc_ref[pl.ds(r0, q), :] + bias_ref[...])
    ldt = jnp.log(dt)
    dta = dt * (-jnp.exp(alog_ref[...]))
    cum_f = sum(_dot(lo_tri, h) for h in _split3(dta))
    cum_b = sum(_dot(up_tri, h) for h in _split3(dta))
    col = lax.broadcasted_iota(jnp.int32, dta.shape, 1)
    return jnp.where(col < nh, cum_f, cum_b), ldt


def _dt_terms_row(dtr_ref, bias_ref, alog_ref, r0, lo_tri, up_tri, nh):
    q = SSD_CHUNK
    dt = _softplus(dtr_ref[:, pl.ds(r0, q)] + bias_ref[...])
    dta = dt * (-jnp.exp(alog_ref[...]))
    cum_f = sum(_dot(h, up_tri) for h in _split3(dta))
    cum_b = sum(_dot(h, lo_tri) for h in _split3(dta))
    row = lax.broadcasted_iota(jnp.int32, dta.shape, 0)
    return jnp.where(row < nh, cum_f, cum_b) - jnp.log(dt)


def _ssd_kernel(xs_ref, b_ref, c_ref, z_ref, dtc_ref, dtr_ref, biasc_ref, biasr_ref,
                alogc_ref, alogr_ref, cwx_ref, cwb_ref, cwc_ref, cbx_ref, cbb_ref, cbc_ref,
                d_ref, gn_ref, o_ref, xc_ref, bv_ref, cv_ref, y_ref, sf_ref, sb_ref):
    q = SSD_CHUNK
    seqlen, width = xs_ref.shape
    n_chunks = seqlen // q
    nh = width // SSD_HEAD_DIM
    n_pairs = nh // 2
    half = SSD_HEAD_DIM

    t_idx = lax.broadcasted_iota(jnp.int32, (q, q), 0)
    s_idx = lax.broadcasted_iota(jnp.int32, (q, q), 1)
    lower = s_idx <= t_idx
    upper = s_idx >= t_idx
    lo_tri = jnp.where(lower, 1.0, 0.0).astype(BF16)
    up_tri = jnp.where(upper, 1.0, 0.0).astype(BF16)
    lane = lax.broadcasted_iota(jnp.int32, (q, LANES), 1)
    lane_lo = lane < half
    lane_row_lo = lax.broadcasted_iota(jnp.int32, (1, LANES), 1) < half
    srow = lax.broadcasted_iota(jnp.int32, (2 * D_STATE, LANES), 0)
    slane = lax.broadcasted_iota(jnp.int32, (2 * D_STATE, LANES), 1)
    diag_blocks = (srow < D_STATE) == (slane < half)

    sf_ref[...] = jnp.zeros_like(sf_ref)
    sb_ref[...] = jnp.zeros_like(sb_ref)

    def carry_update(s_ref, p, bw_pair, xp_bf, decay_row):
        prod = _dot_tn(bw_pair, xp_bf)
        s_ref[p] = s_ref[p] * decay_row + jnp.where(diag_blocks, prod, 0.0)

    def forward_chunk(c, _):
        r0 = pl.multiple_of(c * q, q)
        rows = pl.ds(r0, q)
        xq = _conv_silu_chunk(xs_ref, cwx_ref, cbx_ref, c, n_chunks)
        bq = _conv_silu_chunk(b_ref, cwb_ref, cbb_ref, c, n_chunks)
        cq = _conv_silu_chunk(c_ref, cwc_ref, cbc_ref, c, n_chunks)
        xc_ref[rows, :] = xq.astype(BF16)
        bv_ref[rows, :] = bq
        cv_ref[rows, :] = cq
        cum, ldt = _dt_terms_col(dtc_ref, biasc_ref, alogc_ref, r0, lo_tri, up_tri, nh)
        r_row = _dt_terms_row(dtr_ref, biasr_ref, alogr_ref, r0, lo_tri, up_tri, nh)
        cb = _dot_nt(cq.astype(BF16), bq.astype(BF16))
        for p in range(n_pairs):
            ms, ces, bws, decays = [], [], [], []
            for k in (2 * p, 2 * p + 1):
                cf = cum[:, k:k + 1]
                cr = cum[:, nh + k:nh + k + 1]
                e_f = jnp.where(lower, cf - r_row[k:k + 1, :], -jnp.inf)
                e_b = jnp.where(upper, cr - r_row[nh + k:nh + k + 1, :], -jnp.inf)
                ms.append((cb * (jnp.exp(e_f) + jnp.exp(e_b))).astype(BF16))
                ces.append((cq * jnp.exp(cf)).astype(BF16))
                end = cum[q - 1:q, k:k + 1]
                bws.append((bq * jnp.exp(end - cf + ldt[:, k:k + 1])).astype(BF16))
                decays.append(jnp.exp(end))
            cols = slice(p * LANES, (p + 1) * LANES)
            xp = xq[:, cols]
            xp_bf = xp.astype(BF16)
            zero = jnp.zeros_like(xp_bf)
            lhs = jnp.concatenate(ms + ces, axis=1)
            rhs = jnp.concatenate([jnp.where(lane_lo, xp_bf, zero), jnp.where(lane_lo, zero, xp_bf),
                                   sf_ref[p].astype(BF16)], axis=0)
            y_ref[rows, cols] = _dot(lhs, rhs)
            carry_update(sf_ref, p, jnp.concatenate(bws, axis=1), xp_bf,
                         jnp.where(lane_row_lo, decays[0], decays[1]))
        return 0

    def backward_chunk(i, _):
        c = n_chunks - 1 - i
        r0 = pl.multiple_of(c * q, q)
        rows = pl.ds(r0, q)
        bq = bv_ref[rows, :]
        cq = cv_ref[rows, :]
        cum, ldt = _dt_terms_col(dtc_ref, biasc_ref, alogc_ref, r0, lo_tri, up_tri, nh)
        ssq = jnp.zeros((q, 1), F32)
        for p in range(n_pairs):
            ces, bws, decays = [], [], []
            for k in (2 * p, 2 * p + 1):
                cr = cum[:, nh + k:nh + k + 1]
                ces.append((cq * jnp.exp(cr)).astype(BF16))
                start = cum[0:1, nh + k:nh + k + 1]
                bws.append((bq * jnp.exp(start - cr + ldt[:, nh + k:nh + k + 1])).astype(BF16))
                decays.append(jnp.exp(start))
            cols = slice(p * LANES, (p + 1) * LANES)
            xp_bf = xc_ref[rows, cols]
            y = y_ref[rows, cols] + _dot(jnp.concatenate(ces, axis=1), sb_ref[p].astype(BF16))
            carry_update(sb_ref, p, jnp.concatenate(bws, axis=1), xp_bf,
                         jnp.where(lane_row_lo, decays[0], decays[1]))
            zz = z_ref[rows, cols].astype(F32)
            y = (y + d_ref[:, cols] * xp_bf.astype(F32)) * (zz * _sigmoid(zz))
            y_ref[rows, cols] = y
            ssq = ssq + jnp.sum(y * y, axis=-1, keepdims=True)
        inv = lax.rsqrt(ssq * (1.0 / width) + EPS)
        for p in range(n_pairs):
            cols = slice(p * LANES, (p + 1) * LANES)
            o_ref[rows, cols] = (y_ref[rows, cols] * inv * gn_ref[:, cols]).astype(o_ref.dtype)
        return 0

    lax.fori_loop(0, n_chunks, forward_chunk, 0)
    lax.fori_loop(0, n_chunks, backward_chunk, 0)


def _ssd_mixer(xs_raw, bc_raw, z, dt_raw, conv_w, conv_b, a_log, dt_bias, d_skip, gnorm_w,
               batch, seqlen):
    t, d_ssd = xs_raw.shape
    g = SSD_GROUPS
    gw = d_ssd // g
    kh = gw // SSD_HEAD_DIM
    n = D_STATE
    dtc = dt_raw.reshape(batch, seqlen, 2, g, kh).transpose(0, 3, 1, 2, 4).reshape(batch, g, seqlen, 2 * kh)
    dtr = dtc.transpose(0, 1, 3, 2)
    per_group = lambda v: v.reshape(2, g, kh).transpose(1, 0, 2).reshape(g, 2 * kh)
    bias_c = per_group(dt_bias)[:, None, :]
    bias_r = per_group(dt_bias)[:, :, None]
    alog_c = per_group(a_log)[:, None, :]
    alog_r = per_group(a_log)[:, :, None]
    cw_x, cw_bc = conv_w[:, :d_ssd], conv_w[:, d_ssd:]
    cb_x, cb_bc = conv_b[None, :d_ssd], conv_b[None, d_ssd:]
    d_full = jnp.repeat(d_skip, SSD_HEAD_DIM)[None, :]
    slab = lambda w: pl.BlockSpec((seqlen, w), lambda b, gi: (b, gi))
    return pl.pallas_call(
        _ssd_kernel,
        out_shape=jax.ShapeDtypeStruct((t, d_ssd), BF16),
        grid=(batch, g),
        in_specs=[slab(gw),
                  slab(n),
                  pl.BlockSpec((seqlen, n), lambda b, gi: (b, g + gi)),
                  slab(gw),
                  pl.BlockSpec((None, None, seqlen, 2 * kh), lambda b, gi: (b, gi, 0, 0)),
                  pl.BlockSpec((None, None, 2 * kh, seqlen), lambda b, gi: (b, gi, 0, 0)),
                  pl.BlockSpec((None, 1, 2 * kh), lambda b, gi: (gi, 0, 0)),
                  pl.BlockSpec((None, 2 * kh, 1), lambda b, gi: (gi, 0, 0)),
                  pl.BlockSpec((None, 1, 2 * kh), lambda b, gi: (gi, 0, 0)),
                  pl.BlockSpec((None, 2 * kh, 1), lambda b, gi: (gi, 0, 0)),
                  pl.BlockSpec((D_CONV, gw), lambda b, gi: (0, gi)),
                  pl.BlockSpec((D_CONV, n), lambda b, gi: (0, gi)),
                  pl.BlockSpec((D_CONV, n), lambda b, gi: (0, g + gi)),
                  pl.BlockSpec((1, gw), lambda b, gi: (0, gi)),
                  pl.BlockSpec((1, n), lambda b, gi: (0, gi)),
                  pl.BlockSpec((1, n), lambda b, gi: (0, g + gi)),
                  pl.BlockSpec((1, gw), lambda b, gi: (0, gi)),
                  pl.BlockSpec((1, gw), lambda b, gi: (0, gi))],
        out_specs=slab(gw),
        scratch_shapes=[pltpu.VMEM((seqlen, gw), BF16),
                        pltpu.VMEM((seqlen, n), F32),
                        pltpu.VMEM((seqlen, n), F32),
                        pltpu.VMEM((seqlen, gw), F32),
                        pltpu.VMEM((kh // 2, 2 * n, LANES), F32),
                        pltpu.VMEM((kh // 2, 2 * n, LANES), F32)],
        compiler_params=_params("arbitrary", "arbitrary"),
        name="ssd",
    )(xs_raw, bc_raw, bc_raw, z, dtc, dtr, bias_c, bias_r, alog_c, alog_r,
      cw_x, cw_bc, cw_bc, cb_x, cb_bc, cb_bc, d_full, gnorm_w[None, :])


def _rope_tables(seqlen):
    pos = jnp.arange(seqlen)
    inv_freq = ROPE_THETA ** (-jnp.arange(0, ROPE_HALF, 2, dtype=F32) / ROPE_HALF)
    ang_row = (pos // GRID_W).astype(F32)[:, None] * inv_freq
    ang_col = (pos % GRID_W).astype(F32)[:, None] * inv_freq
    cos = jnp.concatenate([jnp.cos(ang_row)] * 2 + [jnp.cos(ang_col)] * 2, axis=1)
    sin = jnp.concatenate([-jnp.sin(ang_row), jnp.sin(ang_row),
                           -jnp.sin(ang_col), jnp.sin(ang_col)], axis=1)
    return cos, sin


def _qkv_kernel(a_ref, w_ref, nw_ref, sc_ref, cos_ref, sin_ref, o_ref, *, n_rope_tiles):
    acc = _dot(a_ref[...], w_ref[...])
    j = pl.program_id(1)

    @pl.when(j < n_rope_tiles)
    def _():
        cos = cos_ref[...]
        sin = sin_ref[...]
        lane = lax.broadcasted_iota(jnp.int32, cos.shape, 1)
        first_half = (lane % (ROPE_HALF)) < ROPE_HALF // 2
        for h in range(acc.shape[1] // HEAD_DIM):
            cols = slice(h * HEAD_DIM, (h + 1) * HEAD_DIM)
            xh = acc[:, cols]
            ms = jnp.mean(xh * xh, axis=-1, keepdims=True)
            y = xh * lax.rsqrt(ms + EPS) * nw_ref[:, cols]
            partner = jnp.where(first_half, pltpu.roll(y, HEAD_DIM - ROPE_HALF // 2, 1),
                                pltpu.roll(y, ROPE_HALF // 2, 1))
            o_ref[:, cols] = ((y * cos + partner * sin) * sc_ref[:, cols]).astype(o_ref.dtype)

    @pl.when(j >= n_rope_tiles)
    def _():
        o_ref[...] = acc.astype(o_ref.dtype)


def _qkv_proj(hn, w, q_norm, k_norm, seqlen, *, tm=1024, tn=512):
    t, k = hn.shape
    n = w.shape[1]
    tm = min(tm, seqlen)
    nq, nk = N_HEADS * HEAD_DIM, N_KV_HEADS * HEAD_DIM
    norm_row = jnp.concatenate([jnp.tile(q_norm, N_HEADS), jnp.tile(k_norm, N_KV_HEADS),
                                jnp.ones((nk,), F32)])[None, :]
    scale_row = jnp.concatenate([jnp.full((nq,), HEAD_DIM ** -0.5, F32),
                                 jnp.ones((2 * nk,), F32)])[None, :]
    cos, sin = _rope_tables(seqlen)
    row_tiles = seqlen // tm
    return pl.pallas_call(
        functools.partial(_qkv_kernel, n_rope_tiles=(nq + nk) // tn),
        out_shape=jax.ShapeDtypeStruct((t, n), BF16),
        grid=(t // tm, n // tn),
        in_specs=[pl.BlockSpec((tm, k), lambda i, j: (i, 0)),
                  pl.BlockSpec((k, tn), lambda i, j: (0, j)),
                  pl.BlockSpec((1, tn), lambda i, j: (0, j)),
                  pl.BlockSpec((1, tn), lambda i, j: (0, j)),
                  pl.BlockSpec((tm, HEAD_DIM), lambda i, j: (i % row_tiles, 0)),
                  pl.BlockSpec((tm, HEAD_DIM), lambda i, j: (i % row_tiles, 0))],
        out_specs=pl.BlockSpec((tm, tn), lambda i, j: (i, j)),
        compiler_params=_params("arbitrary", "arbitrary"),
        name="qkv_proj",
    )(hn, w, norm_row, scale_row, cos, sin)


def _attn_kernel(q_ref, k_ref, v_ref, o_ref, *, tq):
    seqlen = k_ref.shape[0]
    k = k_ref[...]
    v = v_ref[...]

    def q_block(c, _):
        rows = pl.ds(pl.multiple_of(c * tq, tq), tq)
        for g in range(KV_GROUP):
            cols = slice(g * HEAD_DIM, (g + 1) * HEAD_DIM)
            s = _dot_nt(q_ref[rows, cols], k)
            p = jnp.exp(s - jnp.max(s, axis=-1, keepdims=True))
            denom = jnp.sum(p, axis=-1, keepdims=True)
            o_ref[rows, cols] = (_dot(p.astype(BF16), v) / denom).astype(o_ref.dtype)
        return 0

    lax.fori_loop(0, seqlen // tq, q_block, 0)


def _attention(qkv, batch, seqlen, *, tq=256):
    t = qkv.shape[0]
    gw = KV_GROUP * HEAD_DIM
    k_off = N_HEADS
    v_off = N_HEADS + N_KV_HEADS
    return pl.pallas_call(
        functools.partial(_attn_kernel, tq=min(tq, seqlen)),
        out_shape=jax.ShapeDtypeStruct((t, N_HEADS * HEAD_DIM), BF16),
        grid=(batch, N_KV_HEADS),
        in_specs=[pl.BlockSpec((seqlen, gw), lambda b, h: (b, h)),
                  pl.BlockSpec((seqlen, HEAD_DIM), lambda b, h: (b, k_off + h)),
                  pl.BlockSpec((seqlen, HEAD_DIM), lambda b, h: (b, v_off + h))],
        out_specs=pl.BlockSpec((seqlen, gw), lambda b, h: (b, h)),
        compiler_params=_params("arbitrary", "arbitrary"),
        name="attention",
    )(qkv, qkv, qkv)


def kernel(x, ffn_norm, ffn_w_gate, ffn_w_up, ffn_w_down, mix_norm, hyb_in_proj,
           ssd_conv_w, ssd_conv_b, ssd_A_log, ssd_dt_bias, ssd_D, ssd_gnorm,
           hyb_out_proj, attn_w_qkv, attn_q_norm, attn_k_norm, attn_w_o, final_norm):
    b, l, d = x.shape
    t = b * l
    x = x.reshape(t, d)

    def ffn_half(x, i, h):
        hn = _rmsnorm(x, ffn_norm[i, h])
        return _ffn(hn, x, ffn_w_gate[i, h].astype(BF16), ffn_w_up[i, h].astype(BF16),
                    ffn_w_down[i, h].astype(BF16))

    x = ffn_half(x, 0, 0)
    hn = _rmsnorm(x, mix_norm[0])
    w_in = hyb_in_proj[0]
    d_ssd = ssd_gnorm.shape[1]
    d_fourier = 2 * d - d_ssd
    n_bc = 2 * SSD_GROUPS * D_STATE
    o1 = d_fourier
    o2 = o1 + d_ssd
    o3 = o2 + d_ssd
    o4 = o3 + n_bc
    n_dt = w_in.shape[1] - o4
    w_dt = jnp.pad(w_in[:, o4:], ((0, 0), (0, 2 * LANES - n_dt)))
    u = _matmul(hn, w_in[:, :o1].astype(BF16), BF16)
    z = _matmul(hn, w_in[:, o1:o2].astype(BF16), BF16)
    xs_raw = _matmul(hn, w_in[:, o2:o3].astype(BF16), F32)
    bc_raw = _matmul(hn, w_in[:, o3:o4].astype(BF16), F32)
    dt_raw = _matmul(hn, w_dt.astype(BF16), F32)[:, :n_dt]
    y_f = _fourier_mix(u, b, l)
    y_s = _ssd_mixer(xs_raw, bc_raw, z, dt_raw, ssd_conv_w[0], ssd_conv_b[0], ssd_A_log[0],
                     ssd_dt_bias[0], ssd_D[0], ssd_gnorm[0], b, l)
    x = _matmul2_res(y_f, y_s, hyb_out_proj[0].astype(BF16), x)
    x = ffn_half(x, 0, 1)

    x = ffn_half(x, 1, 0)
    hn = _rmsnorm(x, mix_norm[1])
    qkv = _qkv_proj(hn, attn_w_qkv[0].astype(BF16), attn_q_norm[0], attn_k_norm[0], l)
    o = _attention(qkv, b, l)
    x = _matmul_res(o, attn_w_o[0].astype(BF16), x)
    x = ffn_half(x, 1, 1)

    return _rmsnorm(x, final_norm, F32).reshape(b, l, d)
```

```python
import functools
import math

import jax
import jax.numpy as jnp
from jax import lax
from jax.experimental import pallas as pl
from jax.experimental.pallas import tpu as pltpu

EPS = 1e-6
GRID_W = 64
FOURIER_GROUPS = 8
SSD_HEAD_DIM = 64
SSD_GROUPS = 8
D_STATE = 128
D_CONV = 5
SSD_CHUNK = 128
N_HEADS = 32
N_KV_HEADS = 8
KV_GROUP = N_HEADS // N_KV_HEADS
HEAD_DIM = 128
ROPE_HALF = HEAD_DIM // 2
ROPE_THETA = 10000.0

LANES = 128
SUBLANES = 8
V7X_VMEM_LIMIT_BYTES = 56 * 1024 * 1024

BF16 = jnp.bfloat16
F32 = jnp.float32


def _params(*semantics):
    return pltpu.CompilerParams(dimension_semantics=semantics,
                                vmem_limit_bytes=V7X_VMEM_LIMIT_BYTES)


def _dot(a, b):
    return jnp.dot(a, b, preferred_element_type=F32)


def _dot_nt(a, b):
    return lax.dot_general(a, b, (((1,), (1,)), ((), ())), preferred_element_type=F32)


def _dot_tn(a, b):
    return lax.dot_general(a, b, (((0,), (0,)), ((), ())), preferred_element_type=F32)


def _sigmoid(x):
    return 1.0 / (1.0 + jnp.exp(-x))


def _rmsnorm_kernel(x_ref, w_ref, o_ref):
    x = x_ref[...]
    ms = jnp.mean(x * x, axis=-1, keepdims=True)
    o_ref[...] = (x * lax.rsqrt(ms + EPS) * w_ref[...]).astype(o_ref.dtype)


def _rmsnorm(x, w, out_dtype=BF16, *, tm=256):
    t, d = x.shape
    return pl.pallas_call(
        _rmsnorm_kernel,
        out_shape=jax.ShapeDtypeStruct((t, d), out_dtype),
        grid=(t // tm,),
        in_specs=[pl.BlockSpec((tm, d), lambda i: (i, 0)),
                  pl.BlockSpec((1, d), lambda i: (0, 0))],
        out_specs=pl.BlockSpec((tm, d), lambda i: (i, 0)),
        compiler_params=_params("arbitrary"),
        name="rmsnorm",
    )(x, w.reshape(1, d))


FFN_DOWN_CHUNK = 512
FFN_TILE_F = 512


def _ffn_kernel(hn_ref, x_hbm, wg_ref, wu_ref, wd_ref, o_ref, x_sem):
    i = pl.program_id(0)
    j = pl.program_id(1)
    tm = o_ref.shape[0]

    def x_copy():
        return pltpu.make_async_copy(x_hbm.at[pl.ds(i * tm, tm), :], o_ref, x_sem)

    @pl.when(j == 0)
    def _():
        x_copy().start()

    hn = hn_ref[...]
    g = _dot(hn, wg_ref[...])
    u = _dot(hn, wu_ref[...])
    a = (0.5 * g / (1.0 + jnp.exp(-g)) * u).astype(BF16)

    @pl.when(j == 0)
    def _():
        x_copy().wait()

    for n in range(0, o_ref.shape[1], FFN_DOWN_CHUNK):
        cols = slice(n, n + FFN_DOWN_CHUNK)
        o_ref[:, cols] += _dot(a, wd_ref[:, cols])


def _ffn(hn, x, wg, wu, wd, *, tm=512, tf=512):
    t, d = x.shape
    f = wg.shape[1]
    tm = min(tm, t)
    assert f % tf == 0, (f, tf)
    return pl.pallas_call(
        _ffn_kernel,
        out_shape=jax.ShapeDtypeStruct((t, d), F32),
        grid=(t // tm, f // tf),
        in_specs=[pl.BlockSpec((tm, d), lambda i, j: (i, 0)),
                  pl.BlockSpec(memory_space=pl.ANY),
                  pl.BlockSpec((d, tf), lambda i, j: (0, j)),
                  pl.BlockSpec((d, tf), lambda i, j: (0, j)),
                  pl.BlockSpec((tf, d), lambda i, j: (j, 0))],
        out_specs=pl.BlockSpec((tm, d), lambda i, j: (i, 0)),
        scratch_shapes=[pltpu.SemaphoreType.DMA(())],
        compiler_params=_params("arbitrary", "arbitrary"),
        name="ffn",
    )(hn, x, wg, wu, wd)


def _mm_kernel(a_ref, w_ref, o_ref):
    o_ref[...] = _dot(a_ref[...], w_ref[...]).astype(o_ref.dtype)


def _matmul(a, w, out_dtype, *, tm=1024, tn=1024):
    t, k = a.shape
    n = w.shape[1]
    tm, tn = min(tm, t), min(tn, n)
    return pl.pallas_call(
        _mm_kernel,
        out_shape=jax.ShapeDtypeStruct((t, n), out_dtype),
        grid=(t // tm, n // tn),
        in_specs=[pl.BlockSpec((tm, k), lambda i, j: (i, 0)),
                  pl.BlockSpec((k, tn), lambda i, j: (0, j))],
        out_specs=pl.BlockSpec((tm, tn), lambda i, j: (i, j)),
        compiler_params=_params("arbitrary", "arbitrary"),
        name="matmul",
    )(a, w)


def _mm_res_kernel(a_ref, w_ref, x_ref, o_ref):
    o_ref[...] = x_ref[...] + _dot(a_ref[...], w_ref[...])


def _matmul_res(a, w, x, *, tm=1024, tn=1024):
    t, k = a.shape
    n = w.shape[1]
    tm, tn = min(tm, t), min(tn, n)
    return pl.pallas_call(
        _mm_res_kernel,
        out_shape=jax.ShapeDtypeStruct((t, n), F32),
        grid=(t // tm, n // tn),
        in_specs=[pl.BlockSpec((tm, k), lambda i, j: (i, 0)),
                  pl.BlockSpec((k, tn), lambda i, j: (0, j)),
                  pl.BlockSpec((tm, tn), lambda i, j: (i, j))],
        out_specs=pl.BlockSpec((tm, tn), lambda i, j: (i, j)),
        compiler_params=_params("arbitrary", "arbitrary"),
        name="matmul_res",
    )(a, w, x)


def _mm2_res_kernel(a0_ref, a1_ref, w_ref, x_ref, o_ref):
    k = pl.program_id(2)

    @pl.when(k == 0)
    def _():
        o_ref[...] = x_ref[...] + _dot(a0_ref[...], w_ref[...])

    @pl.when(k > 0)
    def _():
        o_ref[...] += _dot(a1_ref[...], w_ref[...])


def _matmul2_res(a0, a1, w, x, *, tm=1024, tn=1024):
    t, tk = a0.shape
    n = w.shape[1]
    tm, tn = min(tm, t), min(tn, n)
    nk = 1 + a1.shape[1] // tk
    return pl.pallas_call(
        _mm2_res_kernel,
        out_shape=jax.ShapeDtypeStruct((t, n), F32),
        grid=(t // tm, n // tn, nk),
        in_specs=[pl.BlockSpec((tm, tk), lambda i, j, k: (i, 0)),
                  pl.BlockSpec((tm, tk), lambda i, j, k: (i, jnp.maximum(k - 1, 0))),
                  pl.BlockSpec((tk, tn), lambda i, j, k: (k, j)),
                  pl.BlockSpec((tm, tn), lambda i, j, k: (i, j))],
        out_specs=pl.BlockSpec((tm, tn), lambda i, j, k: (i, j)),
        compiler_params=_params("arbitrary", "arbitrary", "arbitrary"),
        name="matmul2_res",
    )(a0, a1, w, x)


def _fourier_kernel(u_ref, wc_ref, dl_ref, o_ref, *, scale):
    c = u_ref.shape[1]
    w1 = _dot(u_ref[...], wc_ref[...]).astype(BF16)
    stacked = jnp.concatenate([w1[:, :c], w1[:, c:]], axis=0)
    o_ref[...] = (_dot(dl_ref[...], stacked) * scale).astype(o_ref.dtype)


def _dft_tables(n):
    idx = jnp.arange(n, dtype=jnp.int32)
    ang = ((idx[:, None] * idx[None, :]) % n).astype(F32) * (2.0 * math.pi / n)
    return jnp.cos(ang), jnp.sin(ang)


def _fourier_mix(u, batch, seqlen):
    t, width = u.shape
    c = width // FOURIER_GROUPS
    cc, sc = _dft_tables(c)
    cl, sl = _dft_tables(seqlen)
    wc = jnp.concatenate([cc, sc], axis=1).astype(BF16)
    dl = jnp.concatenate([cl, -sl], axis=1).astype(BF16)
    return pl.pallas_call(
        functools.partial(_fourier_kernel, scale=1.0 / math.sqrt(seqlen * c)),
        out_shape=jax.ShapeDtypeStruct((t, width), BF16),
        grid=(batch, FOURIER_GROUPS),
        in_specs=[pl.BlockSpec((seqlen, c), lambda b, g: (b, g)),
                  pl.BlockSpec((c, 2 * c), lambda b, g: (0, 0)),
                  pl.BlockSpec((seqlen, 2 * seqlen), lambda b, g: (0, 0))],
        out_specs=pl.BlockSpec((seqlen, c), lambda b, g: (b, g)),
        compiler_params=_params("arbitrary", "arbitrary"),
        name="fourier",
    )(u, wc, dl)


def _split3(x):
    h1 = x.astype(BF16)
    r1 = x - h1.astype(F32)
    h2 = r1.astype(BF16)
    h3 = (r1 - h2.astype(F32)).astype(BF16)
    return h1, h2, h3


def _softplus(v):
    return jnp.maximum(v, 0.0) + jnp.log(1.0 + jnp.exp(-jnp.abs(v)))


def _conv_silu_chunk(raw_ref, w_ref, b_ref, win_ref, c, n_chunks):
    q = SSD_CHUNK
    seqlen = raw_ref.shape[0]
    r0 = pl.multiple_of(c * q, q)
    top = raw_ref[pl.ds(pl.multiple_of(jnp.maximum(r0 - SUBLANES, 0), SUBLANES), SUBLANES), :]
    bot = raw_ref[pl.ds(pl.multiple_of(jnp.minimum(r0 + q, seqlen - SUBLANES), SUBLANES), SUBLANES), :]
    win_ref[0:SUBLANES, :] = jnp.where(c > 0, top, 0.0)
    win_ref[SUBLANES:SUBLANES + q, :] = raw_ref[pl.ds(r0, q), :]
    win_ref[SUBLANES + q:2 * SUBLANES + q, :] = jnp.where(c < n_chunks - 1, bot, 0.0)
    acc = b_ref[...]
    pad = (D_CONV - 1) // 2
    for j in range(D_CONV):
        lo = SUBLANES - pad + j
        acc = acc + w_ref[j:j + 1, :] * win_ref[lo:lo + q, :]
    return acc * _sigmoid(acc)


def _dt_tables(dtc_ref, dtr_ref, biasc_ref, biasr_ref, alogc_ref, alogr_ref,
               cum_ref, ecum_ref, dec_ref, rrow_ref, wrow_ref, lo_tri, up_tri, nh):
    q = SSD_CHUNK
    n_chunks = cum_ref.shape[0]
    per = 2 * nh
    dt = _softplus(dtc_ref[...] + biasc_ref[...])
    parts = _split3(dt * (-jnp.exp(alogc_ref[...])))
    cum_f = sum(_dot(lo_tri, h) for h in parts)
    cum_b = sum(_dot(up_tri, h) for h in parts)
    fwd_col = (lax.broadcasted_iota(jnp.int32, cum_f.shape, 1) % per) < nh
    cum = jnp.where(fwd_col, cum_f, cum_b)
    ecum = jnp.exp(cum)
    dec = jnp.exp(jnp.where(fwd_col[:1], cum[q - 1:q, :], cum[0:1, :]))
    dt_r = _softplus(dtr_ref[...] + biasr_ref[...])
    parts_r = _split3(dt_r * (-jnp.exp(alogr_ref[...])))
    cum_fr = sum(_dot(h, up_tri) for h in parts_r)
    cum_br = sum(_dot(h, lo_tri) for h in parts_r)
    fwd_row = (lax.broadcasted_iota(jnp.int32, cum_fr.shape, 0) % per) < nh
    cum_r = jnp.where(fwd_row, cum_fr, cum_br)
    rrow = cum_r - jnp.log(dt_r)
    total_r = jnp.where(fwd_row[:, :1], cum_r[:, q - 1:q], cum_r[:, 0:1])
    wrow = jnp.exp(total_r - cum_r) * dt_r
    for c in range(n_chunks):
        cols = slice(c * per, (c + 1) * per)
        cum_ref[c] = cum[:, cols]
        ecum_ref[c] = ecum[:, cols]
        dec_ref[c] = dec[:, cols]
        rrow_ref[c] = rrow[cols, :]
        wrow_ref[c] = wrow[cols, :]


def _ssd_kernel(xs_ref, b_ref, c_ref, z_ref, dtc_ref, dtr_ref, biasc_ref, biasr_ref,
                alogc_ref, alogr_ref, cwx_ref, cwb_ref, cwc_ref, cbx_ref, cbb_ref, cbc_ref,
                d_ref, gn_ref, o_ref, xc_ref, bt_ref, cv_ref, y_ref, sf_ref, sb_ref,
                cum_ref, ecum_ref, dec_ref, rrow_ref, wrow_ref, winx_ref, winb_ref, winc_ref):
    q = SSD_CHUNK
    seqlen, width = xs_ref.shape
    n_chunks = seqlen // q
    nh = width // SSD_HEAD_DIM
    n_pairs = nh // 2
    half = SSD_HEAD_DIM

    t_idx = lax.broadcasted_iota(jnp.int32, (q, q), 0)
    s_idx = lax.broadcasted_iota(jnp.int32, (q, q), 1)
    lower = s_idx <= t_idx
    upper = s_idx >= t_idx
    lo_tri = jnp.where(lower, 1.0, 0.0).astype(BF16)
    up_tri = jnp.where(upper, 1.0, 0.0).astype(BF16)
    lane = lax.broadcasted_iota(jnp.int32, (q, LANES), 1)
    lane_lo = lane < half
    lane_row_lo = lax.broadcasted_iota(jnp.int32, (1, LANES), 1) < half
    srow = lax.broadcasted_iota(jnp.int32, (2 * D_STATE, LANES), 0)
    slane = lax.broadcasted_iota(jnp.int32, (2 * D_STATE, LANES), 1)
    diag_blocks = (srow < D_STATE) == (slane < half)

    sf_ref[...] = jnp.zeros_like(sf_ref)
    sb_ref[...] = jnp.zeros_like(sb_ref)
    _dt_tables(dtc_ref, dtr_ref, biasc_ref, biasr_ref, alogc_ref, alogr_ref,
               cum_ref, ecum_ref, dec_ref, rrow_ref, wrow_ref, lo_tri, up_tri, nh)

    def carry_update(s_ref, p, bwt_pair, xp_bf, decay_row):
        prod = _dot(bwt_pair, xp_bf)
        s_ref[p] = s_ref[p] * decay_row + jnp.where(diag_blocks, prod, 0.0)

    def forward_chunk(c, _):
        r0 = pl.multiple_of(c * q, q)
        rows = pl.ds(r0, q)
        xq = _conv_silu_chunk(xs_ref, cwx_ref, cbx_ref, winx_ref, c, n_chunks)
        bq = _conv_silu_chunk(b_ref, cwb_ref, cbb_ref, winb_ref, c, n_chunks)
        cq = _conv_silu_chunk(c_ref, cwc_ref, cbc_ref, winc_ref, c, n_chunks)
        bt = bq.T
        xc_ref[rows, :] = xq.astype(BF16)
        bt_ref[c] = bt
        cv_ref[rows, :] = cq
        cum = cum_ref[c]
        ecum = ecum_ref[c]
        dec = dec_ref[c]
        r_row = rrow_ref[c]
        w_row = wrow_ref[c]
        cb = _dot_nt(cq.astype(BF16), bq.astype(BF16))
        for p in range(n_pairs):
            ms, ces, bws, decays = [], [], [], []
            for k in (2 * p, 2 * p + 1):
                e_f = jnp.where(lower, cum[:, k:k + 1] - r_row[k:k + 1, :], -jnp.inf)
                e_b = jnp.where(upper, cum[:, nh + k:nh + k + 1] - r_row[nh + k:nh + k + 1, :], -jnp.inf)
                ms.append((cb * (jnp.exp(e_f) + jnp.exp(e_b))).astype(BF16))
                ces.append((cq * ecum[:, k:k + 1]).astype(BF16))
                bws.append((bt * w_row[k:k + 1, :]).astype(BF16))
                decays.append(dec[:, k:k + 1])
            cols = slice(p * LANES, (p + 1) * LANES)
            xp = xq[:, cols]
            xp_bf = xp.astype(BF16)
            zero = jnp.zeros_like(xp_bf)
            lhs = jnp.concatenate(ms + ces, axis=1)
            rhs = jnp.concatenate([jnp.where(lane_lo, xp_bf, zero), jnp.where(lane_lo, zero, xp_bf),
                                   sf_ref[p].astype(BF16)], axis=0)
            y_ref[rows, cols] = _dot(lhs, rhs)
            carry_update(sf_ref, p, jnp.concatenate(bws, axis=0), xp_bf,
                         jnp.where(lane_row_lo, decays[0], decays[1]))
        return 0

    def backward_chunk(i, _):
        c = n_chunks - 1 - i
        r0 = pl.multiple_of(c * q, q)
        rows = pl.ds(r0, q)
        bt = bt_ref[c]
        cq = cv_ref[rows, :]
        ecum = ecum_ref[c]
        dec = dec_ref[c]
        w_row = wrow_ref[c]
        ssq = jnp.zeros((q, 1), F32)
        for p in range(n_pairs):
            ces, bws, decays = [], [], []
            for k in (2 * p + nh, 2 * p + 1 + nh):
                ces.append((cq * ecum[:, k:k + 1]).astype(BF16))
                bws.append((bt * w_row[k:k + 1, :]).astype(BF16))
                decays.append(dec[:, k:k + 1])
            cols = slice(p * LANES, (p + 1) * LANES)
            xp_bf = xc_ref[rows, cols]
            y = y_ref[rows, cols] + _dot(jnp.concatenate(ces, axis=1), sb_ref[p].astype(BF16))
            carry_update(sb_ref, p, jnp.concatenate(bws, axis=0), xp_bf,
                         jnp.where(lane_row_lo, decays[0], decays[1]))
            zz = z_ref[rows, cols].astype(F32)
            y = (y + d_ref[:, cols] * xp_bf.astype(F32)) * (zz * _sigmoid(zz))
            y_ref[rows, cols] = y
            ssq = ssq + jnp.sum(y * y, axis=-1, keepdims=True)
        inv = lax.rsqrt(ssq * (1.0 / width) + EPS)
        for p in range(n_pairs):
            cols = slice(p * LANES, (p + 1) * LANES)
            o_ref[rows, cols] = (y_ref[rows, cols] * inv * gn_ref[:, cols]).astype(o_ref.dtype)
        return 0

    lax.fori_loop(0, n_chunks, forward_chunk, 0)
    lax.fori_loop(0, n_chunks, backward_chunk, 0)


def _ssd_mixer(xs_raw, bc_raw, z, dt_raw, conv_w, conv_b, a_log, dt_bias, d_skip, gnorm_w,
               batch, seqlen):
    t, d_ssd = xs_raw.shape
    g = SSD_GROUPS
    gw = d_ssd // g
    kh = gw // SSD_HEAD_DIM
    n = D_STATE
    q = SSD_CHUNK
    nc = seqlen // q
    per = 2 * kh
    dt6 = dt_raw.reshape(batch, nc, q, 2, g, kh)
    dtc = dt6.transpose(0, 4, 2, 1, 3, 5).reshape(batch, g, q, nc * per)
    dtr = dtc.transpose(0, 1, 3, 2)
    per_group = lambda v: jnp.tile(v.reshape(2, g, kh).transpose(1, 0, 2).reshape(g, per), (1, nc))
    bias_c = per_group(dt_bias)[:, None, :]
    bias_r = per_group(dt_bias)[:, :, None]
    alog_c = per_group(a_log)[:, None, :]
    alog_r = per_group(a_log)[:, :, None]
    cw_x, cw_bc = conv_w[:, :d_ssd], conv_w[:, d_ssd:]
    cb_x, cb_bc = conv_b[None, :d_ssd], conv_b[None, d_ssd:]
    d_full = jnp.repeat(d_skip, SSD_HEAD_DIM)[None, :]
    slab = lambda w: pl.BlockSpec((seqlen, w), lambda b, gi: (b, gi))
    return pl.pallas_call(
        _ssd_kernel,
        out_shape=jax.ShapeDtypeStruct((t, d_ssd), BF16),
        grid=(batch, g),
        in_specs=[slab(gw),
                  slab(n),
                  pl.BlockSpec((seqlen, n), lambda b, gi: (b, g + gi)),
                  slab(gw),
                  pl.BlockSpec((None, None, q, nc * per), lambda b, gi: (b, gi, 0, 0)),
                  pl.BlockSpec((None, None, nc * per, q), lambda b, gi: (b, gi, 0, 0)),
                  pl.BlockSpec((None, 1, nc * per), lambda b, gi: (gi, 0, 0)),
                  pl.BlockSpec((None, nc * per, 1), lambda b, gi: (gi, 0, 0)),
                  pl.BlockSpec((None, 1, nc * per), lambda b, gi: (gi, 0, 0)),
                  pl.BlockSpec((None, nc * per, 1), lambda b, gi: (gi, 0, 0)),
                  pl.BlockSpec((D_CONV, gw), lambda b, gi: (0, gi)),
                  pl.BlockSpec((D_CONV, n), lambda b, gi: (0, gi)),
                  pl.BlockSpec((D_CONV, n), lambda b, gi: (0, g + gi)),
                  pl.BlockSpec((1, gw), lambda b, gi: (0, gi)),
                  pl.BlockSpec((1, n), lambda b, gi: (0, gi)),
                  pl.BlockSpec((1, n), lambda b, gi: (0, g + gi)),
                  pl.BlockSpec((1, gw), lambda b, gi: (0, gi)),
                  pl.BlockSpec((1, gw), lambda b, gi: (0, gi))],
        out_specs=slab(gw),
        scratch_shapes=[pltpu.VMEM((seqlen, gw), BF16),
                        pltpu.VMEM((nc, n, q), F32),
                        pltpu.VMEM((seqlen, n), F32),
                        pltpu.VMEM((seqlen, gw), F32),
                        pltpu.VMEM((kh // 2, 2 * n, LANES), F32),
                        pltpu.VMEM((kh // 2, 2 * n, LANES), F32),
                        pltpu.VMEM((nc, q, per), F32),
                        pltpu.VMEM((nc, q, per), F32),
                        pltpu.VMEM((nc, 1, per), F32),
                        pltpu.VMEM((nc, per, q), F32),
                        pltpu.VMEM((nc, per, q), F32),
                        pltpu.VMEM((q + 2 * SUBLANES, gw), F32),
                        pltpu.VMEM((q + 2 * SUBLANES, n), F32),
                        pltpu.VMEM((q + 2 * SUBLANES, n), F32)],
        compiler_params=_params("arbitrary", "arbitrary"),
        name="ssd",
    )(xs_raw, bc_raw, bc_raw, z, dtc, dtr, bias_c, bias_r, alog_c, alog_r,
      cw_x, cw_bc, cw_bc, cb_x, cb_bc, cb_bc, d_full, gnorm_w[None, :])


def _pair_layout(v):
    lead = v.shape[:-1]
    v = v.reshape(*lead, -1, 2, 2, ROPE_HALF // 2)
    return jnp.swapaxes(v, -3, -2).reshape(*lead, -1)


def _rope_tables(seqlen):
    pos = jnp.arange(seqlen)
    inv_freq = ROPE_THETA ** (-jnp.arange(0, ROPE_HALF, 2, dtype=F32) / ROPE_HALF)
    ang_row = (pos // GRID_W).astype(F32)[:, None] * inv_freq
    ang_col = (pos % GRID_W).astype(F32)[:, None] * inv_freq
    cos = jnp.concatenate([jnp.cos(ang_row), jnp.cos(ang_col)] * 2, axis=1)
    sin = jnp.concatenate([-jnp.sin(ang_row), -jnp.sin(ang_col),
                           jnp.sin(ang_row), jnp.sin(ang_col)], axis=1)
    return cos, sin


def _qk_kernel(a_ref, w_ref, nw_ref, sc_ref, cos_ref, sin_ref, o_ref):
    acc = _dot(a_ref[...], w_ref[...])

    @pl.when(pl.program_id(1) >= 0)
    def _():
        cos = cos_ref[...]
        sin = sin_ref[...]
        for h in range(acc.shape[1] // HEAD_DIM):
            cols = slice(h * HEAD_DIM, (h + 1) * HEAD_DIM)
            xh = acc[:, cols]
            ms = jnp.mean(xh * xh, axis=-1, keepdims=True)
            y = xh * lax.rsqrt(ms + EPS) * nw_ref[:, cols]
            partner = pltpu.roll(y, HEAD_DIM // 2, 1)
            o_ref[:, cols] = ((y * cos + partner * sin) * sc_ref[:, cols]).astype(o_ref.dtype)


def _qk_proj(hn, w, q_norm, k_norm, seqlen, *, tm=1024, tn=512):
    t, k = hn.shape
    n = w.shape[1]
    tm = min(tm, seqlen)
    nq = N_HEADS * HEAD_DIM
    norm_row = jnp.concatenate([jnp.tile(_pair_layout(q_norm), N_HEADS),
                                jnp.tile(_pair_layout(k_norm), N_KV_HEADS)])[None, :]
    scale_row = jnp.concatenate([jnp.full((nq,), HEAD_DIM ** -0.5, F32),
                                 jnp.ones((n - nq,), F32)])[None, :]
    cos, sin = _rope_tables(seqlen)
    row_tiles = seqlen // tm
    return pl.pallas_call(
        _qk_kernel,
        out_shape=jax.ShapeDtypeStruct((t, n), BF16),
        grid=(t // tm, n // tn),
        in_specs=[pl.BlockSpec((tm, k), lambda i, j: (i, 0)),
                  pl.BlockSpec((k, tn), lambda i, j: (0, j)),
                  pl.BlockSpec((1, tn), lambda i, j: (0, j)),
                  pl.BlockSpec((1, tn), lambda i, j: (0, j)),
                  pl.BlockSpec((tm, HEAD_DIM), lambda i, j: (i % row_tiles, 0)),
                  pl.BlockSpec((tm, HEAD_DIM), lambda i, j: (i % row_tiles, 0))],
        out_specs=pl.BlockSpec((tm, tn), lambda i, j: (i, j)),
        compiler_params=_params("arbitrary", "arbitrary"),
        name="qk_proj",
    )(hn, w, norm_row, scale_row, cos, sin)


def _attn_kernel(q_ref, k_ref, v_ref, o_ref, *, tq):
    seqlen = k_ref.shape[0]
    k = k_ref[...]
    v = v_ref[...]

    def q_block(c, _):
        rows = pl.ds(pl.multiple_of(c * tq, tq), tq)
        for g in range(KV_GROUP):
            cols = slice(g * HEAD_DIM, (g + 1) * HEAD_DIM)
            s = _dot_nt(q_ref[rows, cols], k)
            p = jnp.exp(s - jnp.max(s, axis=-1, keepdims=True))
            denom = jnp.sum(p, axis=-1, keepdims=True)
            o_ref[rows, cols] = (_dot(p.astype(BF16), v) / denom).astype(o_ref.dtype)
        return 0

    lax.fori_loop(0, seqlen // tq, q_block, 0)


def _attention(qk, v, batch, seqlen, *, tq=256):
    t = qk.shape[0]
    gw = KV_GROUP * HEAD_DIM
    k_off = N_HEADS
    return pl.pallas_call(
        functools.partial(_attn_kernel, tq=min(tq, seqlen)),
        out_shape=jax.ShapeDtypeStruct((t, N_HEADS * HEAD_DIM), BF16),
        grid=(batch, N_KV_HEADS),
        in_specs=[pl.BlockSpec((seqlen, gw), lambda b, h: (b, h)),
                  pl.BlockSpec((seqlen, HEAD_DIM), lambda b, h: (b, k_off + h)),
                  pl.BlockSpec((seqlen, HEAD_DIM), lambda b, h: (b, h))],
        out_specs=pl.BlockSpec((seqlen, gw), lambda b, h: (b, h)),
        compiler_params=_params("arbitrary", "arbitrary"),
        name="attention",
    )(qk, qk, v)


def kernel(x, ffn_norm, ffn_w_gate, ffn_w_up, ffn_w_down, mix_norm, hyb_in_proj,
           ssd_conv_w, ssd_conv_b, ssd_A_log, ssd_dt_bias, ssd_D, ssd_gnorm,
           hyb_out_proj, attn_w_qkv, attn_q_norm, attn_k_norm, attn_w_o, final_norm):
    b, l, d = x.shape
    t = b * l
    x = x.reshape(t, d)

    f_pad = -ffn_w_gate.shape[-1] % FFN_TILE_F

    def ffn_half(x, i, h):
        hn = _rmsnorm(x, ffn_norm[i, h])
        wg = jnp.pad(ffn_w_gate[i, h].astype(BF16), ((0, 0), (0, f_pad)))
        wu = jnp.pad(ffn_w_up[i, h].astype(BF16), ((0, 0), (0, f_pad)))
        wd = jnp.pad(ffn_w_down[i, h].astype(BF16), ((0, f_pad), (0, 0)))
        return _ffn(hn, x, wg, wu, wd, tf=FFN_TILE_F)

    x = ffn_half(x, 0, 0)
    hn = _rmsnorm(x, mix_norm[0])
    w_in = hyb_in_proj[0]
    d_ssd = ssd_gnorm.shape[1]
    d_fourier = 2 * d - d_ssd
    n_bc = 2 * SSD_GROUPS * D_STATE
    o1 = d_fourier
    o2 = o1 + d_ssd
    o3 = o2 + d_ssd
    o4 = o3 + n_bc
    n_dt = w_in.shape[1] - o4
    w_dt = jnp.pad(w_in[:, o4:], ((0, 0), (0, 2 * LANES - n_dt)))
    u = _matmul(hn, w_in[:, :o1].astype(BF16), BF16)
    z = _matmul(hn, w_in[:, o1:o2].astype(BF16), BF16)
    xs_raw = _matmul(hn, w_in[:, o2:o3].astype(BF16), F32)
    bc_raw = _matmul(hn, w_in[:, o3:o4].astype(BF16), F32)
    dt_raw = _matmul(hn, w_dt.astype(BF16), F32)[:, :n_dt]
    y_f = _fourier_mix(u, b, l)
    y_s = _ssd_mixer(xs_raw, bc_raw, z, dt_raw, ssd_conv_w[0], ssd_conv_b[0], ssd_A_log[0],
                     ssd_dt_bias[0], ssd_D[0], ssd_gnorm[0], b, l)
    x = _matmul2_res(y_f, y_s, hyb_out_proj[0].astype(BF16), x)
    x = ffn_half(x, 0, 1)

    x = ffn_half(x, 1, 0)
    hn = _rmsnorm(x, mix_norm[1])
    n_qk = (N_HEADS + N_KV_HEADS) * HEAD_DIM
    qk = _qk_proj(hn, _pair_layout(attn_w_qkv[0][:, :n_qk]).astype(BF16), attn_q_norm[0], attn_k_norm[0], l)
    v = _matmul(hn, attn_w_qkv[0][:, n_qk:].astype(BF16), BF16)
    o = _attention(qk, v, b, l)
    x = _matmul_res(o, attn_w_o[0].astype(BF16), x)
    x = ffn_half(x, 1, 1)

    return _rmsnorm(x, final_norm, F32).reshape(b, l, d)
```

```python
import functools
import math

import jax
import jax.numpy as jnp
from jax import lax
from jax.experimental import pallas as pl
from jax.experimental.pallas import tpu as pltpu

EPS = 1e-6
GRID_W = 64
FOURIER_GROUPS = 8
SSD_HEAD_DIM = 64
SSD_GROUPS = 8
D_STATE = 128
D_CONV = 5
SSD_CHUNK = 128
N_HEADS = 32
N_KV_HEADS = 8
KV_GROUP = N_HEADS // N_KV_HEADS
HEAD_DIM = 128
ROPE_HALF = HEAD_DIM // 2
ROPE_THETA = 10000.0

LANES = 128
SUBLANES = 8
V7X_VMEM_LIMIT_BYTES = 56 * 1024 * 1024

BF16 = jnp.bfloat16
F32 = jnp.float32


def _params(*semantics):
    return pltpu.CompilerParams(dimension_semantics=semantics,
                                vmem_limit_bytes=V7X_VMEM_LIMIT_BYTES)


def _dot(a, b):
    return jnp.dot(a, b, preferred_element_type=F32)


def _dot_nt(a, b):
    return lax.dot_general(a, b, (((1,), (1,)), ((), ())), preferred_element_type=F32)


def _dot_tn(a, b):
    return lax.dot_general(a, b, (((0,), (0,)), ((), ())), preferred_element_type=F32)


def _sigmoid(x):
    return 1.0 / (1.0 + jnp.exp(-x))


def _rmsnorm_kernel(x_ref, w_ref, o_ref):
    x = x_ref[...]
    ms = jnp.mean(x * x, axis=-1, keepdims=True)
    o_ref[...] = (x * lax.rsqrt(ms + EPS) * w_ref[...]).astype(o_ref.dtype)


def _rmsnorm(x, w, out_dtype=BF16, *, tm=256):
    t, d = x.shape
    return pl.pallas_call(
        _rmsnorm_kernel,
        out_shape=jax.ShapeDtypeStruct((t, d), out_dtype),
        grid=(t // tm,),
        in_specs=[pl.BlockSpec((tm, d), lambda i: (i, 0)),
                  pl.BlockSpec((1, d), lambda i: (0, 0))],
        out_specs=pl.BlockSpec((tm, d), lambda i: (i, 0)),
        compiler_params=_params("arbitrary"),
        name="rmsnorm",
    )(x, w.reshape(1, d))


FFN_DOWN_CHUNK = 256
FFN_TILE_M = 1024
FFN_TILE_F = 256
FFN_VMEM_LIMIT_BYTES = 60 * 1024 * 1024


def _ffn_kernel(hn_ref, x_hbm, wg_ref, wu_ref, wd_ref, o_ref, x_sem):
    i = pl.program_id(0)
    j = pl.program_id(1)
    tm = o_ref.shape[0]

    def x_copy():
        return pltpu.make_async_copy(x_hbm.at[pl.ds(i * tm, tm), :], o_ref, x_sem)

    @pl.when(j == 0)
    def _():
        x_copy().start()

    hn = hn_ref[...]
    g = _dot(hn, wg_ref[...])
    u = _dot(hn, wu_ref[...])
    a = (0.5 * g / (1.0 + jnp.exp(-g)) * u).astype(BF16)

    @pl.when(j == 0)
    def _():
        x_copy().wait()

    for n in range(0, o_ref.shape[1], FFN_DOWN_CHUNK):
        cols = slice(n, n + FFN_DOWN_CHUNK)
        o_ref[:, cols] += _dot(a, wd_ref[:, cols])


def _ffn(hn, x, wg, wu, wd, *, tm=FFN_TILE_M, tf=FFN_TILE_F):
    t, d = x.shape
    f = wg.shape[1]
    tm = min(tm, t)
    assert f % tf == 0, (f, tf)
    return pl.pallas_call(
        _ffn_kernel,
        out_shape=jax.ShapeDtypeStruct((t, d), F32),
        grid=(t // tm, f // tf),
        in_specs=[pl.BlockSpec((tm, d), lambda i, j: (i, 0), pipeline_mode=pl.Buffered(1)),
                  pl.BlockSpec(memory_space=pl.ANY),
                  pl.BlockSpec((d, tf), lambda i, j: (0, j)),
                  pl.BlockSpec((d, tf), lambda i, j: (0, j)),
                  pl.BlockSpec((tf, d), lambda i, j: (j, 0))],
        out_specs=pl.BlockSpec((tm, d), lambda i, j: (i, 0)),
        scratch_shapes=[pltpu.SemaphoreType.DMA(())],
        compiler_params=pltpu.CompilerParams(dimension_semantics=("arbitrary", "arbitrary"),
                                             vmem_limit_bytes=FFN_VMEM_LIMIT_BYTES),
        name="ffn",
    )(hn, x, wg, wu, wd)


def _mm_kernel(a_ref, w_ref, o_ref):
    o_ref[...] = _dot(a_ref[...], w_ref[...]).astype(o_ref.dtype)


def _matmul(a, w, out_dtype, *, cols=None, tm=1024, tn=1024):
    t, k = a.shape
    c0, c1 = cols if cols is not None else (0, w.shape[1])
    n = c1 - c0
    tm, tn = min(tm, t), min(tn, n)
    assert n % tn == 0 and c0 % tn == 0, (c0, c1, tn)
    j0 = c0 // tn
    return pl.pallas_call(
        _mm_kernel,
        out_shape=jax.ShapeDtypeStruct((t, n), out_dtype),
        grid=(t // tm, n // tn),
        in_specs=[pl.BlockSpec((tm, k), lambda i, j: (i, 0)),
                  pl.BlockSpec((k, tn), lambda i, j: (0, j0 + j))],
        out_specs=pl.BlockSpec((tm, tn), lambda i, j: (i, j)),
        compiler_params=_params("arbitrary", "arbitrary"),
        name="matmul",
    )(a, w)


def _mm_res_kernel(a_ref, w_ref, x_ref, o_ref):
    o_ref[...] = x_ref[...] + _dot(a_ref[...], w_ref[...])


def _matmul_res(a, w, x, *, tm=1024, tn=1024):
    t, k = a.shape
    n = w.shape[1]
    tm, tn = min(tm, t), min(tn, n)
    return pl.pallas_call(
        _mm_res_kernel,
        out_shape=jax.ShapeDtypeStruct((t, n), F32),
        grid=(t // tm, n // tn),
        in_specs=[pl.BlockSpec((tm, k), lambda i, j: (i, 0)),
                  pl.BlockSpec((k, tn), lambda i, j: (0, j)),
                  pl.BlockSpec((tm, tn), lambda i, j: (i, j))],
        out_specs=pl.BlockSpec((tm, tn), lambda i, j: (i, j)),
        compiler_params=_params("arbitrary", "arbitrary"),
        name="matmul_res",
    )(a, w, x)


def _mm2_res_kernel(a0_ref, a1_ref, w_ref, x_ref, o_ref):
    k = pl.program_id(2)

    @pl.when(k == 0)
    def _():
        o_ref[...] = x_ref[...] + _dot(a0_ref[...], w_ref[...])

    @pl.when(k > 0)
    def _():
        o_ref[...] += _dot(a1_ref[...], w_ref[...])


def _matmul2_res(a0, a1, w, x, *, tm=1024, tn=1024):
    t, tk = a0.shape
    n = w.shape[1]
    tm, tn = min(tm, t), min(tn, n)
    nk = 1 + a1.shape[1] // tk
    return pl.pallas_call(
        _mm2_res_kernel,
        out_shape=jax.ShapeDtypeStruct((t, n), F32),
        grid=(t // tm, n // tn, nk),
        in_specs=[pl.BlockSpec((tm, tk), lambda i, j, k: (i, 0)),
                  pl.BlockSpec((tm, tk), lambda i, j, k: (i, jnp.maximum(k - 1, 0))),
                  pl.BlockSpec((tk, tn), lambda i, j, k: (k, j)),
                  pl.BlockSpec((tm, tn), lambda i, j, k: (i, j))],
        out_specs=pl.BlockSpec((tm, tn), lambda i, j, k: (i, j)),
        compiler_params=_params("arbitrary", "arbitrary", "arbitrary"),
        name="matmul2_res",
    )(a0, a1, w, x)


def _fourier_kernel(u_ref, wc_ref, dl_ref, o_ref, *, scale):
    c = u_ref.shape[1]
    w1 = _dot(u_ref[...], wc_ref[...]).astype(BF16)
    stacked = jnp.concatenate([w1[:, :c], w1[:, c:]], axis=0)
    o_ref[...] = (_dot(dl_ref[...], stacked) * scale).astype(o_ref.dtype)


def _dft_tables(n):
    idx = jnp.arange(n, dtype=jnp.int32)
    ang = ((idx[:, None] * idx[None, :]) % n).astype(F32) * (2.0 * math.pi / n)
    return jnp.cos(ang), jnp.sin(ang)


def _fourier_mix(u, batch, seqlen):
    t, width = u.shape
    c = width // FOURIER_GROUPS
    cc, sc = _dft_tables(c)
    cl, sl = _dft_tables(seqlen)
    wc = jnp.concatenate([cc, sc], axis=1).astype(BF16)
    dl = jnp.concatenate([cl, -sl], axis=1).astype(BF16)
    return pl.pallas_call(
        functools.partial(_fourier_kernel, scale=1.0 / math.sqrt(seqlen * c)),
        out_shape=jax.ShapeDtypeStruct((t, width), BF16),
        grid=(batch, FOURIER_GROUPS),
        in_specs=[pl.BlockSpec((seqlen, c), lambda b, g: (b, g)),
                  pl.BlockSpec((c, 2 * c), lambda b, g: (0, 0)),
                  pl.BlockSpec((seqlen, 2 * seqlen), lambda b, g: (0, 0))],
        out_specs=pl.BlockSpec((seqlen, c), lambda b, g: (b, g)),
        compiler_params=_params("arbitrary", "arbitrary"),
        name="fourier",
    )(u, wc, dl)


def _split3(x):
    h1 = x.astype(BF16)
    r1 = x - h1.astype(F32)
    h2 = r1.astype(BF16)
    h3 = (r1 - h2.astype(F32)).astype(BF16)
    return h1, h2, h3


def _softplus(v):
    return jnp.maximum(v, 0.0) + jnp.log(1.0 + jnp.exp(-jnp.abs(v)))


def _conv_silu_chunk(raw_ref, w_ref, b_ref, win_ref, c, n_chunks):
    q = SSD_CHUNK
    seqlen = raw_ref.shape[0]
    r0 = pl.multiple_of(c * q, q)
    top = raw_ref[pl.ds(pl.multiple_of(jnp.maximum(r0 - SUBLANES, 0), SUBLANES), SUBLANES), :]
    bot = raw_ref[pl.ds(pl.multiple_of(jnp.minimum(r0 + q, seqlen - SUBLANES), SUBLANES), SUBLANES), :]
    win_ref[0:SUBLANES, :] = jnp.where(c > 0, top, 0.0)
    win_ref[SUBLANES:SUBLANES + q, :] = raw_ref[pl.ds(r0, q), :]
    win_ref[SUBLANES + q:2 * SUBLANES + q, :] = jnp.where(c < n_chunks - 1, bot, 0.0)
    acc = b_ref[...]
    pad = (D_CONV - 1) // 2
    for j in range(D_CONV):
        lo = SUBLANES - pad + j
        acc = acc + w_ref[j:j + 1, :] * win_ref[lo:lo + q, :]
    return acc * _sigmoid(acc)


def _dt_tables(dtc_ref, dtr_ref, biasc_ref, biasr_ref, alogc_ref, alogr_ref,
               cum_ref, ecum_ref, dec_ref, rrow_ref, wrow_ref, lo_tri, up_tri, nh):
    q = SSD_CHUNK
    n_chunks = cum_ref.shape[0]
    per = 2 * nh
    dt = _softplus(dtc_ref[...] + biasc_ref[...])
    parts = _split3(dt * (-jnp.exp(alogc_ref[...])))
    cum_f = sum(_dot(lo_tri, h) for h in parts)
    cum_b = sum(_dot(up_tri, h) for h in parts)
    fwd_col = (lax.broadcasted_iota(jnp.int32, cum_f.shape, 1) % per) < nh
    cum = jnp.where(fwd_col, cum_f, cum_b)
    ecum = jnp.exp(cum)
    dec = jnp.exp(jnp.where(fwd_col[:1], cum[q - 1:q, :], cum[0:1, :]))
    dt_r = _softplus(dtr_ref[...] + biasr_ref[...])
    parts_r = _split3(dt_r * (-jnp.exp(alogr_ref[...])))
    cum_fr = sum(_dot(h, up_tri) for h in parts_r)
    cum_br = sum(_dot(h, lo_tri) for h in parts_r)
    fwd_row = (lax.broadcasted_iota(jnp.int32, cum_fr.shape, 0) % per) < nh
    cum_r = jnp.where(fwd_row, cum_fr, cum_br)
    rrow = cum_r - jnp.log(dt_r)
    total_r = jnp.where(fwd_row[:, :1], cum_r[:, q - 1:q], cum_r[:, 0:1])
    wrow = jnp.exp(total_r - cum_r) * dt_r
    for c in range(n_chunks):
        cols = slice(c * per, (c + 1) * per)
        cum_ref[c] = cum[:, cols]
        ecum_ref[c] = ecum[:, cols]
        dec_ref[c] = dec[:, cols]
        rrow_ref[c] = rrow[cols, :]
        wrow_ref[c] = wrow[cols, :]


def _ssd_kernel(xs_ref, b_ref, c_ref, z_ref, dtc_ref, dtr_ref, biasc_ref, biasr_ref,
                alogc_ref, alogr_ref, cwx_ref, cwb_ref, cwc_ref, cbx_ref, cbb_ref, cbc_ref,
                d_ref, gn_ref, o_ref, xc_ref, bt_ref, cv_ref, y_ref, sf_ref, sb_ref,
                cum_ref, ecum_ref, dec_ref, rrow_ref, wrow_ref, winx_ref, winb_ref, winc_ref):
    q = SSD_CHUNK
    seqlen, width = xs_ref.shape
    n_chunks = seqlen // q
    nh = width // SSD_HEAD_DIM
    n_pairs = nh // 2
    half = SSD_HEAD_DIM

    t_idx = lax.broadcasted_iota(jnp.int32, (q, q), 0)
    s_idx = lax.broadcasted_iota(jnp.int32, (q, q), 1)
    lower = s_idx <= t_idx
    upper = s_idx >= t_idx
    lo_tri = jnp.where(lower, 1.0, 0.0).astype(BF16)
    up_tri = jnp.where(upper, 1.0, 0.0).astype(BF16)
    lane = lax.broadcasted_iota(jnp.int32, (q, LANES), 1)
    lane_lo = lane < half
    lane_row_lo = lax.broadcasted_iota(jnp.int32, (1, LANES), 1) < half
    srow = lax.broadcasted_iota(jnp.int32, (2 * D_STATE, LANES), 0)
    slane = lax.broadcasted_iota(jnp.int32, (2 * D_STATE, LANES), 1)
    diag_blocks = (srow < D_STATE) == (slane < half)

    sf_ref[...] = jnp.zeros_like(sf_ref)
    sb_ref[...] = jnp.zeros_like(sb_ref)
    _dt_tables(dtc_ref, dtr_ref, biasc_ref, biasr_ref, alogc_ref, alogr_ref,
               cum_ref, ecum_ref, dec_ref, rrow_ref, wrow_ref, lo_tri, up_tri, nh)

    def carry_update(s_ref, p, bwt_pair, xp_bf, decay_row):
        prod = _dot(bwt_pair, xp_bf)
        s_ref[p] = s_ref[p] * decay_row + jnp.where(diag_blocks, prod, 0.0)

    def forward_chunk(c, _):
        r0 = pl.multiple_of(c * q, q)
        rows = pl.ds(r0, q)
        xq = _conv_silu_chunk(xs_ref, cwx_ref, cbx_ref, winx_ref, c, n_chunks)
        bq = _conv_silu_chunk(b_ref, cwb_ref, cbb_ref, winb_ref, c, n_chunks)
        cq = _conv_silu_chunk(c_ref, cwc_ref, cbc_ref, winc_ref, c, n_chunks)
        bt = bq.T
        xc_ref[rows, :] = xq.astype(BF16)
        bt_ref[c] = bt
        cv_ref[rows, :] = cq
        cum = cum_ref[c]
        ecum = ecum_ref[c]
        dec = dec_ref[c]
        r_row = rrow_ref[c]
        w_row = wrow_ref[c]
        cb = _dot_nt(cq.astype(BF16), bq.astype(BF16))
        for p in range(n_pairs):
            ms, ces, bws, decays = [], [], [], []
            for k in (2 * p, 2 * p + 1):
                e_f = jnp.where(lower, cum[:, k:k + 1] - r_row[k:k + 1, :], -jnp.inf)
                e_b = jnp.where(upper, cum[:, nh + k:nh + k + 1] - r_row[nh + k:nh + k + 1, :], -jnp.inf)
                ms.append((cb * (jnp.exp(e_f) + jnp.exp(e_b))).astype(BF16))
                ces.append((cq * ecum[:, k:k + 1]).astype(BF16))
                bws.append((bt * w_row[k:k + 1, :]).astype(BF16))
                decays.append(dec[:, k:k + 1])
            cols = slice(p * LANES, (p + 1) * LANES)
            xp = xq[:, cols]
            xp_bf = xp.astype(BF16)
            zero = jnp.zeros_like(xp_bf)
            lhs = jnp.concatenate(ms + ces, axis=1)
            rhs = jnp.concatenate([jnp.where(lane_lo, xp_bf, zero), jnp.where(lane_lo, zero, xp_bf),
                                   sf_ref[p].astype(BF16)], axis=0)
            y_ref[rows, cols] = _dot(lhs, rhs)
            carry_update(sf_ref, p, jnp.concatenate(bws, axis=0), xp_bf,
                         jnp.where(lane_row_lo, decays[0], decays[1]))
        return 0

    def backward_chunk(i, _):
        c = n_chunks - 1 - i
        r0 = pl.multiple_of(c * q, q)
        rows = pl.ds(r0, q)
        bt = bt_ref[c]
        cq = cv_ref[rows, :]
        ecum = ecum_ref[c]
        dec = dec_ref[c]
        w_row = wrow_ref[c]
        ssq = jnp.zeros((q, 1), F32)
        for p in range(n_pairs):
            ces, bws, decays = [], [], []
            for k in (2 * p + nh, 2 * p + 1 + nh):
                ces.append((cq * ecum[:, k:k + 1]).astype(BF16))
                bws.append((bt * w_row[k:k + 1, :]).astype(BF16))
                decays.append(dec[:, k:k + 1])
            cols = slice(p * LANES, (p + 1) * LANES)
            xp_bf = xc_ref[rows, cols]
            y = y_ref[rows, cols] + _dot(jnp.concatenate(ces, axis=1), sb_ref[p].astype(BF16))
            carry_update(sb_ref, p, jnp.concatenate(bws, axis=0), xp_bf,
                         jnp.where(lane_row_lo, decays[0], decays[1]))
            zz = z_ref[rows, cols].astype(F32)
            y = (y + d_ref[:, cols] * xp_bf.astype(F32)) * (zz * _sigmoid(zz))
            y_ref[rows, cols] = y
            ssq = ssq + jnp.sum(y * y, axis=-1, keepdims=True)
        inv = lax.rsqrt(ssq * (1.0 / width) + EPS)
        for p in range(n_pairs):
            cols = slice(p * LANES, (p + 1) * LANES)
            o_ref[rows, cols] = (y_ref[rows, cols] * inv * gn_ref[:, cols]).astype(o_ref.dtype)
        return 0

    lax.fori_loop(0, n_chunks, forward_chunk, 0)
    lax.fori_loop(0, n_chunks, backward_chunk, 0)


def _ssd_mixer(xs_raw, bc_raw, z, dt_raw, conv_w, conv_b, a_log, dt_bias, d_skip, gnorm_w,
               batch, seqlen):
    t, d_ssd = xs_raw.shape
    g = SSD_GROUPS
    gw = d_ssd // g
    kh = gw // SSD_HEAD_DIM
    n = D_STATE
    q = SSD_CHUNK
    nc = seqlen // q
    per = 2 * kh
    dt6 = dt_raw.reshape(batch, nc, q, 2, g, kh)
    dtc = dt6.transpose(0, 4, 2, 1, 3, 5).reshape(batch, g, q, nc * per)
    dtr = dtc.transpose(0, 1, 3, 2)
    per_group = lambda v: jnp.tile(v.reshape(2, g, kh).transpose(1, 0, 2).reshape(g, per), (1, nc))
    bias_c = per_group(dt_bias)[:, None, :]
    bias_r = per_group(dt_bias)[:, :, None]
    alog_c = per_group(a_log)[:, None, :]
    alog_r = per_group(a_log)[:, :, None]
    cw_x, cw_bc = conv_w[:, :d_ssd], conv_w[:, d_ssd:]
    cb_x, cb_bc = conv_b[None, :d_ssd], conv_b[None, d_ssd:]
    d_full = jnp.repeat(d_skip, SSD_HEAD_DIM)[None, :]
    slab = lambda w: pl.BlockSpec((seqlen, w), lambda b, gi: (b, gi))
    return pl.pallas_call(
        _ssd_kernel,
        out_shape=jax.ShapeDtypeStruct((t, d_ssd), BF16),
        grid=(batch, g),
        in_specs=[slab(gw),
                  slab(n),
                  pl.BlockSpec((seqlen, n), lambda b, gi: (b, g + gi)),
                  slab(gw),
                  pl.BlockSpec((None, None, q, nc * per), lambda b, gi: (b, gi, 0, 0)),
                  pl.BlockSpec((None, None, nc * per, q), lambda b, gi: (b, gi, 0, 0)),
                  pl.BlockSpec((None, 1, nc * per), lambda b, gi: (gi, 0, 0)),
                  pl.BlockSpec((None, nc * per, 1), lambda b, gi: (gi, 0, 0)),
                  pl.BlockSpec((None, 1, nc * per), lambda b, gi: (gi, 0, 0)),
                  pl.BlockSpec((None, nc * per, 1), lambda b, gi: (gi, 0, 0)),
                  pl.BlockSpec((D_CONV, gw), lambda b, gi: (0, gi)),
                  pl.BlockSpec((D_CONV, n), lambda b, gi: (0, gi)),
                  pl.BlockSpec((D_CONV, n), lambda b, gi: (0, g + gi)),
                  pl.BlockSpec((1, gw), lambda b, gi: (0, gi)),
                  pl.BlockSpec((1, n), lambda b, gi: (0, gi)),
                  pl.BlockSpec((1, n), lambda b, gi: (0, g + gi)),
                  pl.BlockSpec((1, gw), lambda b, gi: (0, gi)),
                  pl.BlockSpec((1, gw), lambda b, gi: (0, gi))],
        out_specs=slab(gw),
        scratch_shapes=[pltpu.VMEM((seqlen, gw), BF16),
                        pltpu.VMEM((nc, n, q), F32),
                        pltpu.VMEM((seqlen, n), F32),
                        pltpu.VMEM((seqlen, gw), F32),
                        pltpu.VMEM((kh // 2, 2 * n, LANES), F32),
                        pltpu.VMEM((kh // 2, 2 * n, LANES), F32),
                        pltpu.VMEM((nc, q, per), F32),
                        pltpu.VMEM((nc, q, per), F32),
                        pltpu.VMEM((nc, 1, per), F32),
                        pltpu.VMEM((nc, per, q), F32),
                        pltpu.VMEM((nc, per, q), F32),
                        pltpu.VMEM((q + 2 * SUBLANES, gw), F32),
                        pltpu.VMEM((q + 2 * SUBLANES, n), F32),
                        pltpu.VMEM((q + 2 * SUBLANES, n), F32)],
        compiler_params=_params("arbitrary", "arbitrary"),
        name="ssd",
    )(xs_raw, bc_raw, bc_raw, z, dtc, dtr, bias_c, bias_r, alog_c, alog_r,
      cw_x, cw_bc, cw_bc, cb_x, cb_bc, cb_bc, d_full, gnorm_w[None, :])


def _pair_layout(v):
    lead = v.shape[:-1]
    v = v.reshape(*lead, -1, 2, 2, ROPE_HALF // 2)
    return jnp.swapaxes(v, -3, -2).reshape(*lead, -1)


def _rope_tables(seqlen):
    pos = jnp.arange(seqlen)
    inv_freq = ROPE_THETA ** (-jnp.arange(0, ROPE_HALF, 2, dtype=F32) / ROPE_HALF)
    ang_row = (pos // GRID_W).astype(F32)[:, None] * inv_freq
    ang_col = (pos % GRID_W).astype(F32)[:, None] * inv_freq
    cos = jnp.concatenate([jnp.cos(ang_row), jnp.cos(ang_col)] * 2, axis=1)
    sin = jnp.concatenate([-jnp.sin(ang_row), -jnp.sin(ang_col),
                           jnp.sin(ang_row), jnp.sin(ang_col)], axis=1)
    return cos, sin


def _qk_kernel(a_ref, w_ref, nw_ref, sc_ref, cos_ref, sin_ref, o_ref):
    acc = _dot(a_ref[...], w_ref[...])

    @pl.when(pl.program_id(1) >= 0)
    def _():
        cos = cos_ref[...]
        sin = sin_ref[...]
        for h in range(acc.shape[1] // HEAD_DIM):
            cols = slice(h * HEAD_DIM, (h + 1) * HEAD_DIM)
            xh = acc[:, cols]
            ms = jnp.mean(xh * xh, axis=-1, keepdims=True)
            y = xh * lax.rsqrt(ms + EPS) * nw_ref[:, cols]
            partner = pltpu.roll(y, HEAD_DIM // 2, 1)
            o_ref[:, cols] = ((y * cos + partner * sin) * sc_ref[:, cols]).astype(o_ref.dtype)


def _qk_proj(hn, w, q_norm, k_norm, seqlen, *, tm=1024, tn=512):
    t, k = hn.shape
    n = w.shape[1]
    tm = min(tm, seqlen)
    nq = N_HEADS * HEAD_DIM
    norm_row = jnp.concatenate([jnp.tile(_pair_layout(q_norm), N_HEADS),
                                jnp.tile(_pair_layout(k_norm), N_KV_HEADS)])[None, :]
    scale_row = jnp.concatenate([jnp.full((nq,), HEAD_DIM ** -0.5, F32),
                                 jnp.ones((n - nq,), F32)])[None, :]
    cos, sin = _rope_tables(seqlen)
    row_tiles = seqlen // tm
    return pl.pallas_call(
        _qk_kernel,
        out_shape=jax.ShapeDtypeStruct((t, n), BF16),
        grid=(t // tm, n // tn),
        in_specs=[pl.BlockSpec((tm, k), lambda i, j: (i, 0)),
                  pl.BlockSpec((k, tn), lambda i, j: (0, j)),
                  pl.BlockSpec((1, tn), lambda i, j: (0, j)),
                  pl.BlockSpec((1, tn), lambda i, j: (0, j)),
                  pl.BlockSpec((tm, HEAD_DIM), lambda i, j: (i % row_tiles, 0)),
                  pl.BlockSpec((tm, HEAD_DIM), lambda i, j: (i % row_tiles, 0))],
        out_specs=pl.BlockSpec((tm, tn), lambda i, j: (i, j)),
        compiler_params=_params("arbitrary", "arbitrary"),
        name="qk_proj",
    )(hn, w, norm_row, scale_row, cos, sin)


def _attn_kernel(q_ref, k_ref, v_ref, o_ref, *, tq):
    seqlen = k_ref.shape[0]
    k = k_ref[...]
    v = v_ref[...]

    def q_block(c, _):
        rows = pl.ds(pl.multiple_of(c * tq, tq), tq)
        for g in range(KV_GROUP):
            cols = slice(g * HEAD_DIM, (g + 1) * HEAD_DIM)
            s = _dot_nt(q_ref[rows, cols], k)
            p = jnp.exp(s - jnp.max(s, axis=-1, keepdims=True))
            denom = jnp.sum(p, axis=-1, keepdims=True)
            o_ref[rows, cols] = (_dot(p.astype(BF16), v) / denom).astype(o_ref.dtype)
        return 0

    lax.fori_loop(0, seqlen // tq, q_block, 0)


def _attention(qk, v, batch, seqlen, *, tq=256):
    t = qk.shape[0]
    gw = KV_GROUP * HEAD_DIM
    k_off = N_HEADS
    return pl.pallas_call(
        functools.partial(_attn_kernel, tq=min(tq, seqlen)),
        out_shape=jax.ShapeDtypeStruct((t, N_HEADS * HEAD_DIM), BF16),
        grid=(batch, N_KV_HEADS),
        in_specs=[pl.BlockSpec((seqlen, gw), lambda b, h: (b, h)),
                  pl.BlockSpec((seqlen, HEAD_DIM), lambda b, h: (b, k_off + h)),
                  pl.BlockSpec((seqlen, HEAD_DIM), lambda b, h: (b, h))],
        out_specs=pl.BlockSpec((seqlen, gw), lambda b, h: (b, h)),
        compiler_params=_params("arbitrary", "arbitrary"),
        name="attention",
    )(qk, qk, v)


def kernel(x, ffn_norm, ffn_w_gate, ffn_w_up, ffn_w_down, mix_norm, hyb_in_proj,
           ssd_conv_w, ssd_conv_b, ssd_A_log, ssd_dt_bias, ssd_D, ssd_gnorm,
           hyb_out_proj, attn_w_qkv, attn_q_norm, attn_k_norm, attn_w_o, final_norm):
    b, l, d = x.shape
    t = b * l
    x = x.reshape(t, d)

    f_pad = -ffn_w_gate.shape[-1] % FFN_TILE_F

    def ffn_half(x, i, h):
        hn = _rmsnorm(x, ffn_norm[i, h])
        wg = jnp.pad(ffn_w_gate[i, h].astype(BF16), ((0, 0), (0, f_pad)))
        wu = jnp.pad(ffn_w_up[i, h].astype(BF16), ((0, 0), (0, f_pad)))
        wd = jnp.pad(ffn_w_down[i, h].astype(BF16), ((0, f_pad), (0, 0)))
        return _ffn(hn, x, wg, wu, wd, tf=FFN_TILE_F)

    x = ffn_half(x, 0, 0)
    hn = _rmsnorm(x, mix_norm[0])
    w_in = hyb_in_proj[0]
    d_ssd = ssd_gnorm.shape[1]
    d_fourier = 2 * d - d_ssd
    n_bc = 2 * SSD_GROUPS * D_STATE
    o1 = d_fourier
    o2 = o1 + d_ssd
    o3 = o2 + d_ssd
    o4 = o3 + n_bc
    n_dt = w_in.shape[1] - o4
    w_dt = jnp.pad(w_in[:, o4:], ((0, 0), (0, 2 * LANES - n_dt)))
    w_in_bf = w_in.astype(BF16)
    u = _matmul(hn, w_in_bf, BF16, cols=(0, o1))
    z = _matmul(hn, w_in_bf, BF16, cols=(o1, o2))
    xs_raw = _matmul(hn, w_in_bf, F32, cols=(o2, o3))
    bc_raw = _matmul(hn, w_in_bf, F32, cols=(o3, o4))
    dt_raw = _matmul(hn, w_dt.astype(BF16), F32)[:, :n_dt]
    y_f = _fourier_mix(u, b, l)
    y_s = _ssd_mixer(xs_raw, bc_raw, z, dt_raw, ssd_conv_w[0], ssd_conv_b[0], ssd_A_log[0],
                     ssd_dt_bias[0], ssd_D[0], ssd_gnorm[0], b, l)
    x = _matmul2_res(y_f, y_s, hyb_out_proj[0].astype(BF16), x)
    x = ffn_half(x, 0, 1)

    x = ffn_half(x, 1, 0)
    hn = _rmsnorm(x, mix_norm[1])
    n_qk = (N_HEADS + N_KV_HEADS) * HEAD_DIM
    qk = _qk_proj(hn, _pair_layout(attn_w_qkv[0][:, :n_qk]).astype(BF16), attn_q_norm[0], attn_k_norm[0], l)
    v = _matmul(hn, attn_w_qkv[0][:, n_qk:].astype(BF16), BF16)
    o = _attention(qk, v, b, l)
    x = _matmul_res(o, attn_w_o[0].astype(BF16), x)
    x = ffn_half(x, 1, 1)

    return _rmsnorm(x, final_norm, F32).reshape(b, l, d)
```

```python
import functools
import math

import jax
import jax.numpy as jnp
from jax import lax
from jax.experimental import pallas as pl
from jax.experimental.pallas import tpu as pltpu

EPS = 1e-6
GRID_W = 64
FOURIER_GROUPS = 8
SSD_HEAD_DIM = 64
SSD_GROUPS = 8
D_STATE = 128
D_CONV = 5
SSD_CHUNK = 128
N_HEADS = 32
N_KV_HEADS = 8
KV_GROUP = N_HEADS // N_KV_HEADS
HEAD_DIM = 128
ROPE_HALF = HEAD_DIM // 2
ROPE_THETA = 10000.0

LANES = 128
SUBLANES = 8
V7X_VMEM_LIMIT_BYTES = 56 * 1024 * 1024

BF16 = jnp.bfloat16
F32 = jnp.float32


def _params(*semantics):
    return pltpu.CompilerParams(dimension_semantics=semantics,
                                vmem_limit_bytes=V7X_VMEM_LIMIT_BYTES)


def _dot(a, b):
    return jnp.dot(a, b, preferred_element_type=F32)


def _dot_nt(a, b):
    return lax.dot_general(a, b, (((1,), (1,)), ((), ())), preferred_element_type=F32)


def _dot_tn(a, b):
    return lax.dot_general(a, b, (((0,), (0,)), ((), ())), preferred_element_type=F32)


def _sigmoid(x):
    return 1.0 / (1.0 + jnp.exp(-x))


def _rmsnorm_kernel(x_ref, w_ref, o_ref):
    x = x_ref[...]
    ms = jnp.mean(x * x, axis=-1, keepdims=True)
    o_ref[...] = (x * lax.rsqrt(ms + EPS) * w_ref[...]).astype(o_ref.dtype)


def _rmsnorm(x, w, out_dtype=BF16, *, tm=256):
    t, d = x.shape
    return pl.pallas_call(
        _rmsnorm_kernel,
        out_shape=jax.ShapeDtypeStruct((t, d), out_dtype),
        grid=(t // tm,),
        in_specs=[pl.BlockSpec((tm, d), lambda i: (i, 0)),
                  pl.BlockSpec((1, d), lambda i: (0, 0))],
        out_specs=pl.BlockSpec((tm, d), lambda i: (i, 0)),
        compiler_params=_params("arbitrary"),
        name="rmsnorm",
    )(x, w.reshape(1, d))


FFN_DOWN_CHUNK = 256
FFN_TILE_M = 1024
FFN_TILE_F = 256
FFN_VMEM_LIMIT_BYTES = 60 * 1024 * 1024


def _ffn_kernel(hn_ref, x_hbm, wg_ref, wu_ref, wd_ref, o_ref, x_sem):
    i = pl.program_id(0)
    j = pl.program_id(1)
    tm = o_ref.shape[0]

    def x_copy():
        return pltpu.make_async_copy(x_hbm.at[pl.ds(i * tm, tm), :], o_ref, x_sem)

    @pl.when(j == 0)
    def _():
        x_copy().start()

    hn = hn_ref[...]
    g = _dot(hn, wg_ref[...])
    u = _dot(hn, wu_ref[...])
    a = (0.5 * g / (1.0 + jnp.exp(-g)) * u).astype(BF16)

    @pl.when(j == 0)
    def _():
        x_copy().wait()

    for n in range(0, o_ref.shape[1], FFN_DOWN_CHUNK):
        cols = slice(n, n + FFN_DOWN_CHUNK)
        o_ref[:, cols] += _dot(a, wd_ref[:, cols])


def _ffn(hn, x, wg, wu, wd, layer, half, *, tm=FFN_TILE_M, tf=FFN_TILE_F):
    t, d = x.shape
    f = wg.shape[-1]
    tm = min(tm, t)
    assert f % tf == 0, (f, tf)
    return pl.pallas_call(
        _ffn_kernel,
        out_shape=jax.ShapeDtypeStruct((t, d), F32),
        grid=(t // tm, f // tf),
        in_specs=[pl.BlockSpec((tm, d), lambda i, j: (i, 0), pipeline_mode=pl.Buffered(1)),
                  pl.BlockSpec(memory_space=pl.ANY),
                  pl.BlockSpec((None, None, d, tf), lambda i, j: (layer, half, 0, j)),
                  pl.BlockSpec((None, None, d, tf), lambda i, j: (layer, half, 0, j)),
                  pl.BlockSpec((None, None, tf, d), lambda i, j: (layer, half, j, 0))],
        out_specs=pl.BlockSpec((tm, d), lambda i, j: (i, 0)),
        scratch_shapes=[pltpu.SemaphoreType.DMA(())],
        compiler_params=pltpu.CompilerParams(dimension_semantics=("arbitrary", "arbitrary"),
                                             vmem_limit_bytes=FFN_VMEM_LIMIT_BYTES),
        name="ffn",
    )(hn, x, wg, wu, wd)


def _mm_kernel(a_ref, w_ref, o_ref):
    o_ref[...] = _dot(a_ref[...], w_ref[...]).astype(o_ref.dtype)


def _matmul(a, w, out_dtype, *, cols=None, tm=1024, tn=1024):
    t, k = a.shape
    c0, c1 = cols if cols is not None else (0, w.shape[1])
    n = c1 - c0
    tm, tn = min(tm, t), min(tn, n)
    assert n % tn == 0 and c0 % tn == 0, (c0, c1, tn)
    j0 = c0 // tn
    return pl.pallas_call(
        _mm_kernel,
        out_shape=jax.ShapeDtypeStruct((t, n), out_dtype),
        grid=(t // tm, n // tn),
        in_specs=[pl.BlockSpec((tm, k), lambda i, j: (i, 0)),
                  pl.BlockSpec((k, tn), lambda i, j: (0, j0 + j))],
        out_specs=pl.BlockSpec((tm, tn), lambda i, j: (i, j)),
        compiler_params=_params("arbitrary", "arbitrary"),
        name="matmul",
    )(a, w)


def _mm_res_kernel(a_ref, w_ref, x_ref, o_ref):
    o_ref[...] = x_ref[...] + _dot(a_ref[...], w_ref[...])


def _matmul_res(a, w, x, *, tm=1024, tn=1024):
    t, k = a.shape
    n = w.shape[1]
    tm, tn = min(tm, t), min(tn, n)
    return pl.pallas_call(
        _mm_res_kernel,
        out_shape=jax.ShapeDtypeStruct((t, n), F32),
        grid=(t // tm, n // tn),
        in_specs=[pl.BlockSpec((tm, k), lambda i, j: (i, 0)),
                  pl.BlockSpec((k, tn), lambda i, j: (0, j)),
                  pl.BlockSpec((tm, tn), lambda i, j: (i, j))],
        out_specs=pl.BlockSpec((tm, tn), lambda i, j: (i, j)),
        compiler_params=_params("arbitrary", "arbitrary"),
        name="matmul_res",
    )(a, w, x)


def _mm2_res_kernel(a0_ref, a1_ref, w_ref, x_ref, o_ref):
    k = pl.program_id(2)

    @pl.when(k == 0)
    def _():
        o_ref[...] = x_ref[...] + _dot(a0_ref[...], w_ref[...])

    @pl.when(k > 0)
    def _():
        o_ref[...] += _dot(a1_ref[...], w_ref[...])


def _matmul2_res(a0, a1, w, x, *, tm=1024, tn=1024):
    t, tk = a0.shape
    n = w.shape[1]
    tm, tn = min(tm, t), min(tn, n)
    nk = 1 + a1.shape[1] // tk
    return pl.pallas_call(
        _mm2_res_kernel,
        out_shape=jax.ShapeDtypeStruct((t, n), F32),
        grid=(t // tm, n // tn, nk),
        in_specs=[pl.BlockSpec((tm, tk), lambda i, j, k: (i, 0)),
                  pl.BlockSpec((tm, tk), lambda i, j, k: (i, jnp.maximum(k - 1, 0))),
                  pl.BlockSpec((tk, tn), lambda i, j, k: (k, j)),
                  pl.BlockSpec((tm, tn), lambda i, j, k: (i, j))],
        out_specs=pl.BlockSpec((tm, tn), lambda i, j, k: (i, j)),
        compiler_params=_params("arbitrary", "arbitrary", "arbitrary"),
        name="matmul2_res",
    )(a0, a1, w, x)


def _fourier_kernel(u_ref, wc_ref, dl_ref, o_ref, *, scale):
    c = u_ref.shape[1]
    w1 = _dot(u_ref[...], wc_ref[...]).astype(BF16)
    stacked = jnp.concatenate([w1[:, :c], w1[:, c:]], axis=0)
    o_ref[...] = (_dot(dl_ref[...], stacked) * scale).astype(o_ref.dtype)


def _dft_tables(n):
    idx = jnp.arange(n, dtype=jnp.int32)
    ang = ((idx[:, None] * idx[None, :]) % n).astype(F32) * (2.0 * math.pi / n)
    return jnp.cos(ang), jnp.sin(ang)


def _fourier_mix(u, batch, seqlen):
    t, width = u.shape
    c = width // FOURIER_GROUPS
    cc, sc = _dft_tables(c)
    cl, sl = _dft_tables(seqlen)
    wc = jnp.concatenate([cc, sc], axis=1).astype(BF16)
    dl = jnp.concatenate([cl, -sl], axis=1).astype(BF16)
    return pl.pallas_call(
        functools.partial(_fourier_kernel, scale=1.0 / math.sqrt(seqlen * c)),
        out_shape=jax.ShapeDtypeStruct((t, width), BF16),
        grid=(batch, FOURIER_GROUPS),
        in_specs=[pl.BlockSpec((seqlen, c), lambda b, g: (b, g)),
                  pl.BlockSpec((c, 2 * c), lambda b, g: (0, 0)),
                  pl.BlockSpec((seqlen, 2 * seqlen), lambda b, g: (0, 0))],
        out_specs=pl.BlockSpec((seqlen, c), lambda b, g: (b, g)),
        compiler_params=_params("arbitrary", "arbitrary"),
        name="fourier",
    )(u, wc, dl)


def _split3(x):
    h1 = x.astype(BF16)
    r1 = x - h1.astype(F32)
    h2 = r1.astype(BF16)
    h3 = (r1 - h2.astype(F32)).astype(BF16)
    return h1, h2, h3


def _softplus(v):
    return jnp.maximum(v, 0.0) + jnp.log(1.0 + jnp.exp(-jnp.abs(v)))


def _conv_silu_chunk(raw_ref, w_ref, b_ref, win_ref, c, n_chunks):
    q = SSD_CHUNK
    seqlen = raw_ref.shape[0]
    r0 = pl.multiple_of(c * q, q)
    top = raw_ref[pl.ds(pl.multiple_of(jnp.maximum(r0 - SUBLANES, 0), SUBLANES), SUBLANES), :]
    bot = raw_ref[pl.ds(pl.multiple_of(jnp.minimum(r0 + q, seqlen - SUBLANES), SUBLANES), SUBLANES), :]
    win_ref[0:SUBLANES, :] = jnp.where(c > 0, top, 0.0)
    win_ref[SUBLANES:SUBLANES + q, :] = raw_ref[pl.ds(r0, q), :]
    win_ref[SUBLANES + q:2 * SUBLANES + q, :] = jnp.where(c < n_chunks - 1, bot, 0.0)
    acc = b_ref[...]
    pad = (D_CONV - 1) // 2
    for j in range(D_CONV):
        lo = SUBLANES - pad + j
        acc = acc + w_ref[j:j + 1, :] * win_ref[lo:lo + q, :]
    return acc * _sigmoid(acc)


def _dt_tables(dtc_ref, dtr_ref, biasc_ref, biasr_ref, alogc_ref, alogr_ref,
               cum_ref, ecum_ref, dec_ref, rrow_ref, wrow_ref, lo_tri, up_tri, nh):
    q = SSD_CHUNK
    n_chunks = cum_ref.shape[0]
    per = 2 * nh
    dt = _softplus(dtc_ref[...] + biasc_ref[...])
    parts = _split3(dt * (-jnp.exp(alogc_ref[...])))
    cum_f = sum(_dot(lo_tri, h) for h in parts)
    cum_b = sum(_dot(up_tri, h) for h in parts)
    fwd_col = (lax.broadcasted_iota(jnp.int32, cum_f.shape, 1) % per) < nh
    cum = jnp.where(fwd_col, cum_f, cum_b)
    ecum = jnp.exp(cum)
    dec = jnp.exp(jnp.where(fwd_col[:1], cum[q - 1:q, :], cum[0:1, :]))
    dt_r = _softplus(dtr_ref[...] + biasr_ref[...])
    parts_r = _split3(dt_r * (-jnp.exp(alogr_ref[...])))
    cum_fr = sum(_dot(h, up_tri) for h in parts_r)
    cum_br = sum(_dot(h, lo_tri) for h in parts_r)
    fwd_row = (lax.broadcasted_iota(jnp.int32, cum_fr.shape, 0) % per) < nh
    cum_r = jnp.where(fwd_row, cum_fr, cum_br)
    rrow = cum_r - jnp.log(dt_r)
    total_r = jnp.where(fwd_row[:, :1], cum_r[:, q - 1:q], cum_r[:, 0:1])
    wrow = jnp.exp(total_r - cum_r) * dt_r
    for c in range(n_chunks):
        cols = slice(c * per, (c + 1) * per)
        cum_ref[c] = cum[:, cols]
        ecum_ref[c] = ecum[:, cols]
        dec_ref[c] = dec[:, cols]
        rrow_ref[c] = rrow[cols, :]
        wrow_ref[c] = wrow[cols, :]


def _ssd_kernel(xs_ref, b_ref, c_ref, z_ref, dtc_ref, dtr_ref, biasc_ref, biasr_ref,
                alogc_ref, alogr_ref, cwx_ref, cwb_ref, cwc_ref, cbx_ref, cbb_ref, cbc_ref,
                d_ref, gn_ref, o_ref, xc_ref, bt_ref, cv_ref, y_ref, sf_ref, sb_ref,
                cum_ref, ecum_ref, dec_ref, rrow_ref, wrow_ref, winx_ref, winb_ref, winc_ref):
    q = SSD_CHUNK
    seqlen, width = xs_ref.shape
    n_chunks = seqlen // q
    nh = width // SSD_HEAD_DIM
    n_pairs = nh // 2
    half = SSD_HEAD_DIM

    t_idx = lax.broadcasted_iota(jnp.int32, (q, q), 0)
    s_idx = lax.broadcasted_iota(jnp.int32, (q, q), 1)
    lower = s_idx <= t_idx
    upper = s_idx >= t_idx
    lo_tri = jnp.where(lower, 1.0, 0.0).astype(BF16)
    up_tri = jnp.where(upper, 1.0, 0.0).astype(BF16)
    lane = lax.broadcasted_iota(jnp.int32, (q, LANES), 1)
    lane_lo = lane < half
    lane_row_lo = lax.broadcasted_iota(jnp.int32, (1, LANES), 1) < half
    srow = lax.broadcasted_iota(jnp.int32, (2 * D_STATE, LANES), 0)
    slane = lax.broadcasted_iota(jnp.int32, (2 * D_STATE, LANES), 1)
    diag_blocks = (srow < D_STATE) == (slane < half)

    sf_ref[...] = jnp.zeros_like(sf_ref)
    sb_ref[...] = jnp.zeros_like(sb_ref)
    _dt_tables(dtc_ref, dtr_ref, biasc_ref, biasr_ref, alogc_ref, alogr_ref,
               cum_ref, ecum_ref, dec_ref, rrow_ref, wrow_ref, lo_tri, up_tri, nh)

    def carry_update(s_ref, p, bwt_pair, xp_bf, decay_row):
        prod = _dot(bwt_pair, xp_bf)
        s_ref[p] = s_ref[p] * decay_row + jnp.where(diag_blocks, prod, 0.0)

    def forward_chunk(c, _):
        r0 = pl.multiple_of(c * q, q)
        rows = pl.ds(r0, q)
        xq = _conv_silu_chunk(xs_ref, cwx_ref, cbx_ref, winx_ref, c, n_chunks)
        bq = _conv_silu_chunk(b_ref, cwb_ref, cbb_ref, winb_ref, c, n_chunks)
        cq = _conv_silu_chunk(c_ref, cwc_ref, cbc_ref, winc_ref, c, n_chunks)
        bt = bq.T
        xc_ref[rows, :] = xq.astype(BF16)
        bt_ref[c] = bt
        cv_ref[rows, :] = cq
        cum = cum_ref[c]
        ecum = ecum_ref[c]
        dec = dec_ref[c]
        r_row = rrow_ref[c]
        w_row = wrow_ref[c]
        cb = _dot_nt(cq.astype(BF16), bq.astype(BF16))
        for p in range(n_pairs):
            ms, ces, bws, decays = [], [], [], []
            for k in (2 * p, 2 * p + 1):
                e_f = jnp.where(lower, cum[:, k:k + 1] - r_row[k:k + 1, :], -jnp.inf)
                e_b = jnp.where(upper, cum[:, nh + k:nh + k + 1] - r_row[nh + k:nh + k + 1, :], -jnp.inf)
                ms.append((cb * (jnp.exp(e_f) + jnp.exp(e_b))).astype(BF16))
                ces.append((cq * ecum[:, k:k + 1]).astype(BF16))
                bws.append((bt * w_row[k:k + 1, :]).astype(BF16))
                decays.append(dec[:, k:k + 1])
            cols = slice(p * LANES, (p + 1) * LANES)
            xp = xq[:, cols]
            xp_bf = xp.astype(BF16)
            zero = jnp.zeros_like(xp_bf)
            lhs = jnp.concatenate(ms + ces, axis=1)
            rhs = jnp.concatenate([jnp.where(lane_lo, xp_bf, zero), jnp.where(lane_lo, zero, xp_bf),
                                   sf_ref[p].astype(BF16)], axis=0)
            y_ref[rows, cols] = _dot(lhs, rhs)
            carry_update(sf_ref, p, jnp.concatenate(bws, axis=0), xp_bf,
                         jnp.where(lane_row_lo, decays[0], decays[1]))
        return 0

    def backward_chunk(i, _):
        c = n_chunks - 1 - i
        r0 = pl.multiple_of(c * q, q)
        rows = pl.ds(r0, q)
        bt = bt_ref[c]
        cq = cv_ref[rows, :]
        ecum = ecum_ref[c]
        dec = dec_ref[c]
        w_row = wrow_ref[c]
        ssq = jnp.zeros((q, 1), F32)
        for p in range(n_pairs):
            ces, bws, decays = [], [], []
            for k in (2 * p + nh, 2 * p + 1 + nh):
                ces.append((cq * ecum[:, k:k + 1]).astype(BF16))
                bws.append((bt * w_row[k:k + 1, :]).astype(BF16))
                decays.append(dec[:, k:k + 1])
            cols = slice(p * LANES, (p + 1) * LANES)
            xp_bf = xc_ref[rows, cols]
            y = y_ref[rows, cols] + _dot(jnp.concatenate(ces, axis=1), sb_ref[p].astype(BF16))
            carry_update(sb_ref, p, jnp.concatenate(bws, axis=0), xp_bf,
                         jnp.where(lane_row_lo, decays[0], decays[1]))
            zz = z_ref[rows, cols].astype(F32)
            y = (y + d_ref[:, cols] * xp_bf.astype(F32)) * (zz * _sigmoid(zz))
            y_ref[rows, cols] = y
            ssq = ssq + jnp.sum(y * y, axis=-1, keepdims=True)
        inv = lax.rsqrt(ssq * (1.0 / width) + EPS)
        for p in range(n_pairs):
            cols = slice(p * LANES, (p + 1) * LANES)
            o_ref[rows, cols] = (y_ref[rows, cols] * inv * gn_ref[:, cols]).astype(o_ref.dtype)
        return 0

    lax.fori_loop(0, n_chunks, forward_chunk, 0)
    lax.fori_loop(0, n_chunks, backward_chunk, 0)


def _ssd_mixer(xs_raw, bc_raw, z, dt_raw, conv_w, conv_b, a_log, dt_bias, d_skip, gnorm_w,
               batch, seqlen):
    t, d_ssd = xs_raw.shape
    g = SSD_GROUPS
    gw = d_ssd // g
    kh = gw // SSD_HEAD_DIM
    n = D_STATE
    q = SSD_CHUNK
    nc = seqlen // q
    per = 2 * kh
    dt6 = dt_raw.reshape(batch, nc, q, 2, g, kh)
    dtc = dt6.transpose(0, 4, 2, 1, 3, 5).reshape(batch, g, q, nc * per)
    dtr = dtc.transpose(0, 1, 3, 2)
    per_group = lambda v: jnp.tile(v.reshape(2, g, kh).transpose(1, 0, 2).reshape(g, per), (1, nc))
    bias_c = per_group(dt_bias)[:, None, :]
    bias_r = per_group(dt_bias)[:, :, None]
    alog_c = per_group(a_log)[:, None, :]
    alog_r = per_group(a_log)[:, :, None]
    cw_x, cw_bc = conv_w[:, :d_ssd], conv_w[:, d_ssd:]
    cb_x, cb_bc = conv_b[None, :d_ssd], conv_b[None, d_ssd:]
    d_full = jnp.repeat(d_skip, SSD_HEAD_DIM)[None, :]
    slab = lambda w: pl.BlockSpec((seqlen, w), lambda b, gi: (b, gi))
    return pl.pallas_call(
        _ssd_kernel,
        out_shape=jax.ShapeDtypeStruct((t, d_ssd), BF16),
        grid=(batch, g),
        in_specs=[slab(gw),
                  slab(n),
                  pl.BlockSpec((seqlen, n), lambda b, gi: (b, g + gi)),
                  slab(gw),
                  pl.BlockSpec((None, None, q, nc * per), lambda b, gi: (b, gi, 0, 0)),
                  pl.BlockSpec((None, None, nc * per, q), lambda b, gi: (b, gi, 0, 0)),
                  pl.BlockSpec((None, 1, nc * per), lambda b, gi: (gi, 0, 0)),
                  pl.BlockSpec((None, nc * per, 1), lambda b, gi: (gi, 0, 0)),
                  pl.BlockSpec((None, 1, nc * per), lambda b, gi: (gi, 0, 0)),
                  pl.BlockSpec((None, nc * per, 1), lambda b, gi: (gi, 0, 0)),
                  pl.BlockSpec((D_CONV, gw), lambda b, gi: (0, gi)),
                  pl.BlockSpec((D_CONV, n), lambda b, gi: (0, gi)),
                  pl.BlockSpec((D_CONV, n), lambda b, gi: (0, g + gi)),
                  pl.BlockSpec((1, gw), lambda b, gi: (0, gi)),
                  pl.BlockSpec((1, n), lambda b, gi: (0, gi)),
                  pl.BlockSpec((1, n), lambda b, gi: (0, g + gi)),
                  pl.BlockSpec((1, gw), lambda b, gi: (0, gi)),
                  pl.BlockSpec((1, gw), lambda b, gi: (0, gi))],
        out_specs=slab(gw),
        scratch_shapes=[pltpu.VMEM((seqlen, gw), BF16),
                        pltpu.VMEM((nc, n, q), F32),
                        pltpu.VMEM((seqlen, n), F32),
                        pltpu.VMEM((seqlen, gw), F32),
                        pltpu.VMEM((kh // 2, 2 * n, LANES), F32),
                        pltpu.VMEM((kh // 2, 2 * n, LANES), F32),
                        pltpu.VMEM((nc, q, per), F32),
                        pltpu.VMEM((nc, q, per), F32),
                        pltpu.VMEM((nc, 1, per), F32),
                        pltpu.VMEM((nc, per, q), F32),
                        pltpu.VMEM((nc, per, q), F32),
                        pltpu.VMEM((q + 2 * SUBLANES, gw), F32),
                        pltpu.VMEM((q + 2 * SUBLANES, n), F32),
                        pltpu.VMEM((q + 2 * SUBLANES, n), F32)],
        compiler_params=_params("arbitrary", "arbitrary"),
        name="ssd",
    )(xs_raw, bc_raw, bc_raw, z, dtc, dtr, bias_c, bias_r, alog_c, alog_r,
      cw_x, cw_bc, cw_bc, cb_x, cb_bc, cb_bc, d_full, gnorm_w[None, :])


def _pair_layout(v):
    lead = v.shape[:-1]
    v = v.reshape(*lead, -1, 2, 2, ROPE_HALF // 2)
    return jnp.swapaxes(v, -3, -2).reshape(*lead, -1)


def _rope_tables(seqlen):
    pos = jnp.arange(seqlen)
    inv_freq = ROPE_THETA ** (-jnp.arange(0, ROPE_HALF, 2, dtype=F32) / ROPE_HALF)
    ang_row = (pos // GRID_W).astype(F32)[:, None] * inv_freq
    ang_col = (pos % GRID_W).astype(F32)[:, None] * inv_freq
    cos = jnp.concatenate([jnp.cos(ang_row), jnp.cos(ang_col)] * 2, axis=1)
    sin = jnp.concatenate([-jnp.sin(ang_row), -jnp.sin(ang_col),
                           jnp.sin(ang_row), jnp.sin(ang_col)], axis=1)
    return cos, sin


def _qk_kernel(a_ref, w_ref, nw_ref, sc_ref, cos_ref, sin_ref, o_ref):
    acc = _dot(a_ref[...], w_ref[...])

    @pl.when(pl.program_id(1) >= 0)
    def _():
        cos = cos_ref[...]
        sin = sin_ref[...]
        for h in range(acc.shape[1] // HEAD_DIM):
            cols = slice(h * HEAD_DIM, (h + 1) * HEAD_DIM)
            xh = acc[:, cols]
            ms = jnp.mean(xh * xh, axis=-1, keepdims=True)
            y = xh * lax.rsqrt(ms + EPS) * nw_ref[:, cols]
            partner = pltpu.roll(y, HEAD_DIM // 2, 1)
            o_ref[:, cols] = ((y * cos + partner * sin) * sc_ref[:, cols]).astype(o_ref.dtype)


def _qk_proj(hn, w, q_norm, k_norm, seqlen, *, tm=1024, tn=512):
    t, k = hn.shape
    n = w.shape[1]
    tm = min(tm, seqlen)
    nq = N_HEADS * HEAD_DIM
    norm_row = jnp.concatenate([jnp.tile(_pair_layout(q_norm), N_HEADS),
                                jnp.tile(_pair_layout(k_norm), N_KV_HEADS)])[None, :]
    scale_row = jnp.concatenate([jnp.full((nq,), HEAD_DIM ** -0.5 * math.log2(math.e), F32),
                                 jnp.ones((n - nq,), F32)])[None, :]
    cos, sin = _rope_tables(seqlen)
    row_tiles = seqlen // tm
    return pl.pallas_call(
        _qk_kernel,
        out_shape=jax.ShapeDtypeStruct((t, n), BF16),
        grid=(t // tm, n // tn),
        in_specs=[pl.BlockSpec((tm, k), lambda i, j: (i, 0)),
                  pl.BlockSpec((k, tn), lambda i, j: (0, j)),
                  pl.BlockSpec((1, tn), lambda i, j: (0, j)),
                  pl.BlockSpec((1, tn), lambda i, j: (0, j)),
                  pl.BlockSpec((tm, HEAD_DIM), lambda i, j: (i % row_tiles, 0)),
                  pl.BlockSpec((tm, HEAD_DIM), lambda i, j: (i % row_tiles, 0))],
        out_specs=pl.BlockSpec((tm, tn), lambda i, j: (i, j)),
        compiler_params=_params("arbitrary", "arbitrary"),
        name="qk_proj",
    )(hn, w, norm_row, scale_row, cos, sin)


def _attn_kernel(q_ref, k_ref, v_ref, o_ref, *, tq):
    seqlen = k_ref.shape[0]
    k = k_ref[...]
    v = v_ref[...]

    def q_block(c, _):
        rows = pl.ds(pl.multiple_of(c * tq, tq), tq)
        for g in range(KV_GROUP):
            cols = slice(g * HEAD_DIM, (g + 1) * HEAD_DIM)
            s = _dot_nt(q_ref[rows, cols], k)
            p = jnp.exp2(s - jnp.max(s, axis=-1, keepdims=True))
            denom = jnp.sum(p, axis=-1, keepdims=True)
            o_ref[rows, cols] = (_dot(p.astype(BF16), v) / denom).astype(o_ref.dtype)
        return 0

    lax.fori_loop(0, seqlen // tq, q_block, 0)


def _attention(qk, v, batch, seqlen, *, tq=512):
    t = qk.shape[0]
    gw = KV_GROUP * HEAD_DIM
    k_off = N_HEADS
    return pl.pallas_call(
        functools.partial(_attn_kernel, tq=min(tq, seqlen)),
        out_shape=jax.ShapeDtypeStruct((t, N_HEADS * HEAD_DIM), BF16),
        grid=(batch, N_KV_HEADS),
        in_specs=[pl.BlockSpec((seqlen, gw), lambda b, h: (b, h)),
                  pl.BlockSpec((seqlen, HEAD_DIM), lambda b, h: (b, k_off + h)),
                  pl.BlockSpec((seqlen, HEAD_DIM), lambda b, h: (b, h))],
        out_specs=pl.BlockSpec((seqlen, gw), lambda b, h: (b, h)),
        compiler_params=_params("arbitrary", "arbitrary"),
        name="attention",
    )(qk, qk, v)


def kernel(x, ffn_norm, ffn_w_gate, ffn_w_up, ffn_w_down, mix_norm, hyb_in_proj,
           ssd_conv_w, ssd_conv_b, ssd_A_log, ssd_dt_bias, ssd_D, ssd_gnorm,
           hyb_out_proj, attn_w_qkv, attn_q_norm, attn_k_norm, attn_w_o, final_norm):
    b, l, d = x.shape
    t = b * l
    x = x.reshape(t, d)

    wg_all = ffn_w_gate.astype(BF16)
    wu_all = ffn_w_up.astype(BF16)
    wd_all = ffn_w_down.astype(BF16)

    def ffn_half(x, i, h):
        hn = _rmsnorm(x, ffn_norm[i, h])
        return _ffn(hn, x, wg_all, wu_all, wd_all, i, h)

    x = ffn_half(x, 0, 0)
    hn = _rmsnorm(x, mix_norm[0])
    w_in = hyb_in_proj[0]
    d_ssd = ssd_gnorm.shape[1]
    d_fourier = 2 * d - d_ssd
    n_bc = 2 * SSD_GROUPS * D_STATE
    o1 = d_fourier
    o2 = o1 + d_ssd
    o3 = o2 + d_ssd
    o4 = o3 + n_bc
    n_dt = w_in.shape[1] - o4
    w_dt = jnp.pad(w_in[:, o4:], ((0, 0), (0, 2 * LANES - n_dt)))
    w_in_bf = w_in.astype(BF16)
    u = _matmul(hn, w_in_bf, BF16, cols=(0, o1))
    z = _matmul(hn, w_in_bf, BF16, cols=(o1, o2))
    xs_raw = _matmul(hn, w_in_bf, F32, cols=(o2, o3))
    bc_raw = _matmul(hn, w_in_bf, F32, cols=(o3, o4))
    dt_raw = _matmul(hn, w_dt.astype(BF16), F32)[:, :n_dt]
    y_f = _fourier_mix(u, b, l)
    y_s = _ssd_mixer(xs_raw, bc_raw, z, dt_raw, ssd_conv_w[0], ssd_conv_b[0], ssd_A_log[0],
                     ssd_dt_bias[0], ssd_D[0], ssd_gnorm[0], b, l)
    x = _matmul2_res(y_f, y_s, hyb_out_proj[0].astype(BF16), x)
    x = ffn_half(x, 0, 1)

    x = ffn_half(x, 1, 0)
    hn = _rmsnorm(x, mix_norm[1])
    n_qk = (N_HEADS + N_KV_HEADS) * HEAD_DIM
    qk = _qk_proj(hn, _pair_layout(attn_w_qkv[0][:, :n_qk]).astype(BF16), attn_q_norm[0], attn_k_norm[0], l)
    v = _matmul(hn, attn_w_qkv[0][:, n_qk:].astype(BF16), BF16)
    o = _attention(qk, v, b, l)
    x = _matmul_res(o, attn_w_o[0].astype(BF16), x)
    x = ffn_half(x, 1, 1)

    return _rmsnorm(x, final_norm, F32).reshape(b, l, d)
```

```python
import functools
import math

import jax
import jax.numpy as jnp
from jax import lax
from jax.experimental import pallas as pl
from jax.experimental.pallas import tpu as pltpu

EPS = 1e-6
GRID_W = 64
FOURIER_GROUPS = 8
SSD_HEAD_DIM = 64
SSD_GROUPS = 8
D_STATE = 128
D_CONV = 5
SSD_CHUNK = 128
N_HEADS = 32
N_KV_HEADS = 8
KV_GROUP = N_HEADS // N_KV_HEADS
HEAD_DIM = 128
ROPE_HALF = HEAD_DIM // 2
ROPE_THETA = 10000.0

LANES = 128
SUBLANES = 8
V7X_VMEM_LIMIT_BYTES = 56 * 1024 * 1024

BF16 = jnp.bfloat16
F32 = jnp.float32
LOG2_E = math.log2(math.e)


def _params(*semantics):
    return pltpu.CompilerParams(dimension_semantics=semantics,
                                vmem_limit_bytes=V7X_VMEM_LIMIT_BYTES)


def _dot(a, b):
    return jnp.dot(a, b, preferred_element_type=F32)


def _dot_nt(a, b):
    return lax.dot_general(a, b, (((1,), (1,)), ((), ())), preferred_element_type=F32)


def _dot_tn(a, b):
    return lax.dot_general(a, b, (((0,), (0,)), ((), ())), preferred_element_type=F32)


def _sigmoid(x):
    return 1.0 / (1.0 + jnp.exp(-x))


def _rmsnorm_kernel(x_ref, w_ref, o_ref):
    x = x_ref[...]
    ms = jnp.mean(x * x, axis=-1, keepdims=True)
    o_ref[...] = (x * lax.rsqrt(ms + EPS) * w_ref[...]).astype(o_ref.dtype)


def _rmsnorm(x, w, out_dtype=BF16, *, tm=512):
    t, d = x.shape
    return pl.pallas_call(
        _rmsnorm_kernel,
        out_shape=jax.ShapeDtypeStruct((t, d), out_dtype),
        grid=(t // tm,),
        in_specs=[pl.BlockSpec((tm, d), lambda i: (i, 0)),
                  pl.BlockSpec((1, d), lambda i: (0, 0))],
        out_specs=pl.BlockSpec((tm, d), lambda i: (i, 0)),
        compiler_params=_params("arbitrary"),
        name="rmsnorm",
    )(x, w.reshape(1, d))


FFN_DOWN_CHUNK = 256
FFN_TILE_M = 1024
FFN_TILE_F = 256
FFN_VMEM_LIMIT_BYTES = 60 * 1024 * 1024


def _ffn_kernel(hn_ref, x_hbm, wg_ref, wu_ref, wd_ref, o_ref, x_sem):
    i = pl.program_id(0)
    j = pl.program_id(1)
    tm = o_ref.shape[0]

    def x_copy():
        return pltpu.make_async_copy(x_hbm.at[pl.ds(i * tm, tm), :], o_ref, x_sem)

    @pl.when(j == 0)
    def _():
        x_copy().start()

    hn = hn_ref[...]
    g = _dot(hn, wg_ref[...])
    u = _dot(hn, wu_ref[...])
    a = (0.5 * g / (1.0 + jnp.exp(-g)) * u).astype(BF16)

    @pl.when(j == 0)
    def _():
        x_copy().wait()

    for n in range(0, o_ref.shape[1], FFN_DOWN_CHUNK):
        cols = slice(n, n + FFN_DOWN_CHUNK)
        o_ref[:, cols] += _dot(a, wd_ref[:, cols].astype(BF16))


def _ffn(hn, x, wg, wu, wd, layer, half, *, tm=FFN_TILE_M, tf=FFN_TILE_F):
    t, d = x.shape
    f = wg.shape[-1]
    tm = min(tm, t)
    assert f % tf == 0, (f, tf)
    return pl.pallas_call(
        _ffn_kernel,
        out_shape=jax.ShapeDtypeStruct((t, d), F32),
        grid=(t // tm, f // tf),
        in_specs=[pl.BlockSpec((tm, d), lambda i, j: (i, 0), pipeline_mode=pl.Buffered(1)),
                  pl.BlockSpec(memory_space=pl.ANY),
                  pl.BlockSpec((None, None, d, tf), lambda i, j: (layer, half, 0, j)),
                  pl.BlockSpec((None, None, d, tf), lambda i, j: (layer, half, 0, j)),
                  pl.BlockSpec((None, None, tf, d), lambda i, j: (layer, half, j, 0))],
        out_specs=pl.BlockSpec((tm, d), lambda i, j: (i, 0)),
        scratch_shapes=[pltpu.SemaphoreType.DMA(())],
        compiler_params=pltpu.CompilerParams(dimension_semantics=("arbitrary", "arbitrary"),
                                             vmem_limit_bytes=FFN_VMEM_LIMIT_BYTES),
        name="ffn",
    )(hn, x, wg, wu, wd)


def _mm_kernel(a_ref, w_ref, o_ref):
    o_ref[...] = _dot(a_ref[...], w_ref[...]).astype(o_ref.dtype)


def _matmul(a, w, out_dtype, *, cols=None, tm=1024, tn=1024):
    t, k = a.shape
    c0, c1 = cols if cols is not None else (0, w.shape[1])
    n = c1 - c0
    tm, tn = min(tm, t), min(tn, n)
    assert n % tn == 0 and c0 % tn == 0, (c0, c1, tn)
    j0 = c0 // tn
    return pl.pallas_call(
        _mm_kernel,
        out_shape=jax.ShapeDtypeStruct((t, n), out_dtype),
        grid=(t // tm, n // tn),
        in_specs=[pl.BlockSpec((tm, k), lambda i, j: (i, 0)),
                  pl.BlockSpec((k, tn), lambda i, j: (0, j0 + j))],
        out_specs=pl.BlockSpec((tm, tn), lambda i, j: (i, j)),
        compiler_params=_params("arbitrary", "arbitrary"),
        name="matmul",
    )(a, w)


def _mm_res_kernel(a_ref, w_ref, x_ref, o_ref):
    o_ref[...] = x_ref[...] + _dot(a_ref[...], w_ref[...])


def _matmul_res(a, w, x, *, tm=1024, tn=1024):
    t, k = a.shape
    n = w.shape[1]
    tm, tn = min(tm, t), min(tn, n)
    return pl.pallas_call(
        _mm_res_kernel,
        out_shape=jax.ShapeDtypeStruct((t, n), F32),
        grid=(t // tm, n // tn),
        in_specs=[pl.BlockSpec((tm, k), lambda i, j: (i, 0)),
                  pl.BlockSpec((k, tn), lambda i, j: (0, j)),
                  pl.BlockSpec((tm, tn), lambda i, j: (i, j))],
        out_specs=pl.BlockSpec((tm, tn), lambda i, j: (i, j)),
        compiler_params=_params("arbitrary", "arbitrary"),
        name="matmul_res",
    )(a, w, x)


def _mm2_res_kernel(a0_ref, a1_ref, w_ref, x_ref, o_ref):
    k = pl.program_id(2)

    @pl.when(k == 0)
    def _():
        o_ref[...] = x_ref[...] + _dot(a0_ref[...], w_ref[...])

    @pl.when(k > 0)
    def _():
        o_ref[...] += _dot(a1_ref[...], w_ref[...])


def _matmul2_res(a0, a1, w, x, *, tm=1024, tn=1024):
    t, tk = a0.shape
    n = w.shape[1]
    tm, tn = min(tm, t), min(tn, n)
    nk = 1 + a1.shape[1] // tk
    return pl.pallas_call(
        _mm2_res_kernel,
        out_shape=jax.ShapeDtypeStruct((t, n), F32),
        grid=(t // tm, n // tn, nk),
        in_specs=[pl.BlockSpec((tm, tk), lambda i, j, k: (i, 0)),
                  pl.BlockSpec((tm, tk), lambda i, j, k: (i, jnp.maximum(k - 1, 0))),
                  pl.BlockSpec((tk, tn), lambda i, j, k: (k, j)),
                  pl.BlockSpec((tm, tn), lambda i, j, k: (i, j))],
        out_specs=pl.BlockSpec((tm, tn), lambda i, j, k: (i, j)),
        compiler_params=_params("arbitrary", "arbitrary", "arbitrary"),
        name="matmul2_res",
    )(a0, a1, w, x)


def _fourier_kernel(u_ref, wc_ref, dl_ref, o_ref, *, scale):
    c = u_ref.shape[1]
    w1 = _dot(u_ref[...], wc_ref[...]).astype(BF16)
    stacked = jnp.concatenate([w1[:, :c], w1[:, c:]], axis=0)
    o_ref[...] = (_dot(dl_ref[...], stacked) * scale).astype(o_ref.dtype)


def _dft_tables(n):
    idx = jnp.arange(n, dtype=jnp.int32)
    ang = ((idx[:, None] * idx[None, :]) % n).astype(F32) * (2.0 * math.pi / n)
    return jnp.cos(ang), jnp.sin(ang)


def _fourier_mix(u, batch, seqlen):
    t, width = u.shape
    c = width // FOURIER_GROUPS
    cc, sc = _dft_tables(c)
    cl, sl = _dft_tables(seqlen)
    wc = jnp.concatenate([cc, sc], axis=1).astype(BF16)
    dl = jnp.concatenate([cl, -sl], axis=1).astype(BF16)
    return pl.pallas_call(
        functools.partial(_fourier_kernel, scale=1.0 / math.sqrt(seqlen * c)),
        out_shape=jax.ShapeDtypeStruct((t, width), BF16),
        grid=(batch, FOURIER_GROUPS),
        in_specs=[pl.BlockSpec((seqlen, c), lambda b, g: (b, g)),
                  pl.BlockSpec((c, 2 * c), lambda b, g: (0, 0)),
                  pl.BlockSpec((seqlen, 2 * seqlen), lambda b, g: (0, 0))],
        out_specs=pl.BlockSpec((seqlen, c), lambda b, g: (b, g)),
        compiler_params=_params("arbitrary", "arbitrary"),
        name="fourier",
    )(u, wc, dl)


def _split3(x):
    h1 = x.astype(BF16)
    r1 = x - h1.astype(F32)
    h2 = r1.astype(BF16)
    h3 = (r1 - h2.astype(F32)).astype(BF16)
    return h1, h2, h3


def _softplus(v):
    return jnp.maximum(v, 0.0) + jnp.log(1.0 + jnp.exp(-jnp.abs(v)))


def _conv_silu_chunk(raw_ref, w_ref, b_ref, win_ref, c, n_chunks):
    q = SSD_CHUNK
    seqlen = raw_ref.shape[0]
    r0 = pl.multiple_of(c * q, q)
    top = raw_ref[pl.ds(pl.multiple_of(jnp.maximum(r0 - SUBLANES, 0), SUBLANES), SUBLANES), :]
    bot = raw_ref[pl.ds(pl.multiple_of(jnp.minimum(r0 + q, seqlen - SUBLANES), SUBLANES), SUBLANES), :]
    win_ref[0:SUBLANES, :] = jnp.where(c > 0, top, 0.0)
    win_ref[SUBLANES:SUBLANES + q, :] = raw_ref[pl.ds(r0, q), :]
    win_ref[SUBLANES + q:2 * SUBLANES + q, :] = jnp.where(c < n_chunks - 1, bot, 0.0)
    acc = b_ref[...]
    pad = (D_CONV - 1) // 2
    for j in range(D_CONV):
        lo = SUBLANES - pad + j
        acc = acc + w_ref[j:j + 1, :] * win_ref[lo:lo + q, :]
    return acc * _sigmoid(acc)


def _dt_tables(dtc_ref, dtr_ref, biasc_ref, biasr_ref, alogc_ref, alogr_ref,
               cum_ref, ecum_ref, dec_ref, rrow_ref, wrow_ref, lo_tri, up_tri, nh):
    q = SSD_CHUNK
    n_chunks = cum_ref.shape[0]
    per = 2 * nh
    dt = _softplus(dtc_ref[...] + biasc_ref[...])
    parts = _split3(dt * (-jnp.exp(alogc_ref[...])))
    cum_f = sum(_dot(lo_tri, h) for h in parts)
    cum_b = sum(_dot(up_tri, h) for h in parts)
    fwd_col = (lax.broadcasted_iota(jnp.int32, cum_f.shape, 1) % per) < nh
    cum = jnp.where(fwd_col, cum_f, cum_b)
    ecum = jnp.exp(cum)
    dec = jnp.exp(jnp.where(fwd_col[:1], cum[q - 1:q, :], cum[0:1, :]))
    dt_r = _softplus(dtr_ref[...] + biasr_ref[...])
    parts_r = _split3(dt_r * (-jnp.exp(alogr_ref[...])))
    cum_fr = sum(_dot(h, up_tri) for h in parts_r)
    cum_br = sum(_dot(h, lo_tri) for h in parts_r)
    fwd_row = (lax.broadcasted_iota(jnp.int32, cum_fr.shape, 0) % per) < nh
    cum_r = jnp.where(fwd_row, cum_fr, cum_br)
    rrow = cum_r - jnp.log(dt_r)
    total_r = jnp.where(fwd_row[:, :1], cum_r[:, q - 1:q], cum_r[:, 0:1])
    wrow = jnp.exp(total_r - cum_r) * dt_r
    cum = cum * LOG2_E
    rrow = rrow * LOG2_E
    for c in range(n_chunks):
        cols = slice(c * per, (c + 1) * per)
        cum_ref[c] = cum[:, cols]
        ecum_ref[c] = ecum[:, cols]
        dec_ref[c] = dec[:, cols]
        rrow_ref[c] = rrow[cols, :]
        wrow_ref[c] = wrow[cols, :]


def _ssd_kernel(xs_ref, b_ref, c_ref, z_ref, dtc_ref, dtr_ref, biasc_ref, biasr_ref,
                alogc_ref, alogr_ref, cwx_ref, cwb_ref, cwc_ref, cbx_ref, cbb_ref, cbc_ref,
                d_ref, gn_ref, o_ref, xc_ref, bt_ref, cv_ref, y_ref, sf_ref, sb_ref,
                cum_ref, ecum_ref, dec_ref, rrow_ref, wrow_ref, winx_ref, winb_ref, winc_ref):
    q = SSD_CHUNK
    seqlen, width = xs_ref.shape
    n_chunks = seqlen // q
    nh = width // SSD_HEAD_DIM
    n_pairs = nh // 2
    half = SSD_HEAD_DIM

    t_idx = lax.broadcasted_iota(jnp.int32, (q, q), 0)
    s_idx = lax.broadcasted_iota(jnp.int32, (q, q), 1)
    lower = s_idx <= t_idx
    upper = s_idx >= t_idx
    lo_tri = jnp.where(lower, 1.0, 0.0).astype(BF16)
    up_tri = jnp.where(upper, 1.0, 0.0).astype(BF16)
    lane = lax.broadcasted_iota(jnp.int32, (q, LANES), 1)
    lane_lo = lane < half
    lane_row_lo = lax.broadcasted_iota(jnp.int32, (1, LANES), 1) < half
    srow = lax.broadcasted_iota(jnp.int32, (2 * D_STATE, LANES), 0)
    slane = lax.broadcasted_iota(jnp.int32, (2 * D_STATE, LANES), 1)
    diag_blocks = (srow < D_STATE) == (slane < half)

    sf_ref[...] = jnp.zeros_like(sf_ref)
    sb_ref[...] = jnp.zeros_like(sb_ref)
    _dt_tables(dtc_ref, dtr_ref, biasc_ref, biasr_ref, alogc_ref, alogr_ref,
               cum_ref, ecum_ref, dec_ref, rrow_ref, wrow_ref, lo_tri, up_tri, nh)

    def carry_update(s_ref, p, bwt_pair, xp_bf, decay_row):
        prod = _dot(bwt_pair, xp_bf)
        s_ref[p] = s_ref[p] * decay_row + jnp.where(diag_blocks, prod, 0.0)

    def forward_chunk(c, _):
        r0 = pl.multiple_of(c * q, q)
        rows = pl.ds(r0, q)
        xq = _conv_silu_chunk(xs_ref, cwx_ref, cbx_ref, winx_ref, c, n_chunks)
        bq = _conv_silu_chunk(b_ref, cwb_ref, cbb_ref, winb_ref, c, n_chunks)
        cq = _conv_silu_chunk(c_ref, cwc_ref, cbc_ref, winc_ref, c, n_chunks)
        bt = bq.T
        xc_ref[rows, :] = xq.astype(BF16)
        bt_ref[c] = bt
        cv_ref[rows, :] = cq
        cum = cum_ref[c]
        ecum = ecum_ref[c]
        dec = dec_ref[c]
        r_row = rrow_ref[c]
        w_row = wrow_ref[c]
        cb = _dot_nt(cq.astype(BF16), bq.astype(BF16))
        for p in range(n_pairs):
            ms, ces, bws, decays = [], [], [], []
            for k in (2 * p, 2 * p + 1):
                e_f = jnp.where(lower, cum[:, k:k + 1] - r_row[k:k + 1, :], -jnp.inf)
                e_b = jnp.where(upper, cum[:, nh + k:nh + k + 1] - r_row[nh + k:nh + k + 1, :], -jnp.inf)
                ms.append((cb * (jnp.exp2(e_f) + jnp.exp2(e_b))).astype(BF16))
                ces.append((cq * ecum[:, k:k + 1]).astype(BF16))
                bws.append((bt * w_row[k:k + 1, :]).astype(BF16))
                decays.append(dec[:, k:k + 1])
            cols = slice(p * LANES, (p + 1) * LANES)
            xp = xq[:, cols]
            xp_bf = xp.astype(BF16)
            zero = jnp.zeros_like(xp_bf)
            lhs = jnp.concatenate(ms + ces, axis=1)
            rhs = jnp.concatenate([jnp.where(lane_lo, xp_bf, zero), jnp.where(lane_lo, zero, xp_bf),
                                   sf_ref[p].astype(BF16)], axis=0)
            y_ref[rows, cols] = _dot(lhs, rhs)
            carry_update(sf_ref, p, jnp.concatenate(bws, axis=0), xp_bf,
                         jnp.where(lane_row_lo, decays[0], decays[1]))
        return 0

    def backward_chunk(i, _):
        c = n_chunks - 1 - i
        r0 = pl.multiple_of(c * q, q)
        rows = pl.ds(r0, q)
        bt = bt_ref[c]
        cq = cv_ref[rows, :]
        ecum = ecum_ref[c]
        dec = dec_ref[c]
        w_row = wrow_ref[c]
        ssq = jnp.zeros((q, 1), F32)
        for p in range(n_pairs):
            ces, bws, decays = [], [], []
            for k in (2 * p + nh, 2 * p + 1 + nh):
                ces.append((cq * ecum[:, k:k + 1]).astype(BF16))
                bws.append((bt * w_row[k:k + 1, :]).astype(BF16))
                decays.append(dec[:, k:k + 1])
            cols = slice(p * LANES, (p + 1) * LANES)
            xp_bf = xc_ref[rows, cols]
            y = y_ref[rows, cols] + _dot(jnp.concatenate(ces, axis=1), sb_ref[p].astype(BF16))
            carry_update(sb_ref, p, jnp.concatenate(bws, axis=0), xp_bf,
                         jnp.where(lane_row_lo, decays[0], decays[1]))
            zz = z_ref[rows, cols].astype(F32)
            y = (y + d_ref[:, cols] * xp_bf.astype(F32)) * (zz * _sigmoid(zz))
            y_ref[rows, cols] = y
            ssq = ssq + jnp.sum(y * y, axis=-1, keepdims=True)
        inv = lax.rsqrt(ssq * (1.0 / width) + EPS)
        for p in range(n_pairs):
            cols = slice(p * LANES, (p + 1) * LANES)
            o_ref[rows, cols] = (y_ref[rows, cols] * inv * gn_ref[:, cols]).astype(o_ref.dtype)
        return 0

    lax.fori_loop(0, n_chunks, forward_chunk, 0)
    lax.fori_loop(0, n_chunks, backward_chunk, 0)


def _ssd_mixer(xs_raw, bc_raw, z, dt_raw, conv_w, conv_b, a_log, dt_bias, d_skip, gnorm_w,
               batch, seqlen):
    t, d_ssd = xs_raw.shape
    g = SSD_GROUPS
    gw = d_ssd // g
    kh = gw // SSD_HEAD_DIM
    n = D_STATE
    q = SSD_CHUNK
    nc = seqlen // q
    per = 2 * kh
    dt6 = dt_raw.reshape(batch, nc, q, 2, g, kh)
    dtc = dt6.transpose(0, 4, 2, 1, 3, 5).reshape(batch, g, q, nc * per)
    dtr = dtc.transpose(0, 1, 3, 2)
    per_group = lambda v: jnp.tile(v.reshape(2, g, kh).transpose(1, 0, 2).reshape(g, per), (1, nc))
    bias_c = per_group(dt_bias)[:, None, :]
    bias_r = per_group(dt_bias)[:, :, None]
    alog_c = per_group(a_log)[:, None, :]
    alog_r = per_group(a_log)[:, :, None]
    cw_x, cw_bc = conv_w[:, :d_ssd], conv_w[:, d_ssd:]
    cb_x, cb_bc = conv_b[None, :d_ssd], conv_b[None, d_ssd:]
    d_full = jnp.repeat(d_skip, SSD_HEAD_DIM)[None, :]
    slab = lambda w: pl.BlockSpec((seqlen, w), lambda b, gi: (b, gi))
    return pl.pallas_call(
        _ssd_kernel,
        out_shape=jax.ShapeDtypeStruct((t, d_ssd), BF16),
        grid=(batch, g),
        in_specs=[slab(gw),
                  slab(n),
                  pl.BlockSpec((seqlen, n), lambda b, gi: (b, g + gi)),
                  slab(gw),
                  pl.BlockSpec((None, None, q, nc * per), lambda b, gi: (b, gi, 0, 0)),
                  pl.BlockSpec((None, None, nc * per, q), lambda b, gi: (b, gi, 0, 0)),
                  pl.BlockSpec((None, 1, nc * per), lambda b, gi: (gi, 0, 0)),
                  pl.BlockSpec((None, nc * per, 1), lambda b, gi: (gi, 0, 0)),
                  pl.BlockSpec((None, 1, nc * per), lambda b, gi: (gi, 0, 0)),
                  pl.BlockSpec((None, nc * per, 1), lambda b, gi: (gi, 0, 0)),
                  pl.BlockSpec((D_CONV, gw), lambda b, gi: (0, gi)),
                  pl.BlockSpec((D_CONV, n), lambda b, gi: (0, gi)),
                  pl.BlockSpec((D_CONV, n), lambda b, gi: (0, g + gi)),
                  pl.BlockSpec((1, gw), lambda b, gi: (0, gi)),
                  pl.BlockSpec((1, n), lambda b, gi: (0, gi)),
                  pl.BlockSpec((1, n), lambda b, gi: (0, g + gi)),
                  pl.BlockSpec((1, gw), lambda b, gi: (0, gi)),
                  pl.BlockSpec((1, gw), lambda b, gi: (0, gi))],
        out_specs=slab(gw),
        scratch_shapes=[pltpu.VMEM((seqlen, gw), BF16),
                        pltpu.VMEM((nc, n, q), F32),
                        pltpu.VMEM((seqlen, n), F32),
                        pltpu.VMEM((seqlen, gw), F32),
                        pltpu.VMEM((kh // 2, 2 * n, LANES), F32),
                        pltpu.VMEM((kh // 2, 2 * n, LANES), F32),
                        pltpu.VMEM((nc, q, per), F32),
                        pltpu.VMEM((nc, q, per), F32),
                        pltpu.VMEM((nc, 1, per), F32),
                        pltpu.VMEM((nc, per, q), F32),
                        pltpu.VMEM((nc, per, q), F32),
                        pltpu.VMEM((q + 2 * SUBLANES, gw), F32),
                        pltpu.VMEM((q + 2 * SUBLANES, n), F32),
                        pltpu.VMEM((q + 2 * SUBLANES, n), F32)],
        compiler_params=_params("arbitrary", "arbitrary"),
        name="ssd",
    )(xs_raw, bc_raw, bc_raw, z, dtc, dtr, bias_c, bias_r, alog_c, alog_r,
      cw_x, cw_bc, cw_bc, cb_x, cb_bc, cb_bc, d_full, gnorm_w[None, :])


def _pair_layout(v):
    lead = v.shape[:-1]
    v = v.reshape(*lead, -1, 2, 2, ROPE_HALF // 2)
    return jnp.swapaxes(v, -3, -2).reshape(*lead, -1)


def _rope_tables(seqlen):
    pos = jnp.arange(seqlen)
    inv_freq = ROPE_THETA ** (-jnp.arange(0, ROPE_HALF, 2, dtype=F32) / ROPE_HALF)
    ang_row = (pos // GRID_W).astype(F32)[:, None] * inv_freq
    ang_col = (pos % GRID_W).astype(F32)[:, None] * inv_freq
    cos = jnp.concatenate([jnp.cos(ang_row), jnp.cos(ang_col)] * 2, axis=1)
    sin = jnp.concatenate([-jnp.sin(ang_row), -jnp.sin(ang_col),
                           jnp.sin(ang_row), jnp.sin(ang_col)], axis=1)
    return cos, sin


def _qk_kernel(a_ref, w_ref, nw_ref, sc_ref, cos_ref, sin_ref, o_ref):
    acc = _dot(a_ref[...], w_ref[...])

    @pl.when(pl.program_id(1) >= 0)
    def _():
        cos = cos_ref[...]
        sin = sin_ref[...]
        for h in range(acc.shape[1] // HEAD_DIM):
            cols = slice(h * HEAD_DIM, (h + 1) * HEAD_DIM)
            xh = acc[:, cols]
            ms = jnp.mean(xh * xh, axis=-1, keepdims=True)
            y = xh * lax.rsqrt(ms + EPS) * nw_ref[:, cols]
            partner = pltpu.roll(y, HEAD_DIM // 2, 1)
            o_ref[:, cols] = ((y * cos + partner * sin) * sc_ref[:, cols]).astype(o_ref.dtype)


def _qk_proj(hn, w, q_norm, k_norm, seqlen, *, tm=1024, tn=512):
    t, k = hn.shape
    n = w.shape[1]
    tm = min(tm, seqlen)
    nq = N_HEADS * HEAD_DIM
    norm_row = jnp.concatenate([jnp.tile(_pair_layout(q_norm), N_HEADS),
                                jnp.tile(_pair_layout(k_norm), N_KV_HEADS)])[None, :]
    scale_row = jnp.concatenate([jnp.full((nq,), HEAD_DIM ** -0.5 * math.log2(math.e), F32),
                                 jnp.ones((n - nq,), F32)])[None, :]
    cos, sin = _rope_tables(seqlen)
    row_tiles = seqlen // tm
    return pl.pallas_call(
        _qk_kernel,
        out_shape=jax.ShapeDtypeStruct((t, n), BF16),
        grid=(t // tm, n // tn),
        in_specs=[pl.BlockSpec((tm, k), lambda i, j: (i, 0)),
                  pl.BlockSpec((k, tn), lambda i, j: (0, j)),
                  pl.BlockSpec((1, tn), lambda i, j: (0, j)),
                  pl.BlockSpec((1, tn), lambda i, j: (0, j)),
                  pl.BlockSpec((tm, HEAD_DIM), lambda i, j: (i % row_tiles, 0)),
                  pl.BlockSpec((tm, HEAD_DIM), lambda i, j: (i % row_tiles, 0))],
        out_specs=pl.BlockSpec((tm, tn), lambda i, j: (i, j)),
        compiler_params=_params("arbitrary", "arbitrary"),
        name="qk_proj",
    )(hn, w, norm_row, scale_row, cos, sin)


def _attn_kernel(q_ref, k_ref, v_ref, o_ref, *, tq):
    seqlen = k_ref.shape[0]
    k = k_ref[...]
    v = v_ref[...]

    def q_block(c, _):
        rows = pl.ds(pl.multiple_of(c * tq, tq), tq)
        for g in range(KV_GROUP):
            cols = slice(g * HEAD_DIM, (g + 1) * HEAD_DIM)
            s = _dot_nt(q_ref[rows, cols], k)
            p = jnp.exp2(s - jnp.max(s, axis=-1, keepdims=True))
            denom = jnp.sum(p, axis=-1, keepdims=True)
            o_ref[rows, cols] = (_dot(p.astype(BF16), v) / denom).astype(o_ref.dtype)
        return 0

    lax.fori_loop(0, seqlen // tq, q_block, 0)


def _attention(qk, v, batch, seqlen, *, tq=512):
    t = qk.shape[0]
    gw = KV_GROUP * HEAD_DIM
    k_off = N_HEADS
    return pl.pallas_call(
        functools.partial(_attn_kernel, tq=min(tq, seqlen)),
        out_shape=jax.ShapeDtypeStruct((t, N_HEADS * HEAD_DIM), BF16),
        grid=(batch, N_KV_HEADS),
        in_specs=[pl.BlockSpec((seqlen, gw), lambda b, h: (b, h)),
                  pl.BlockSpec((seqlen, HEAD_DIM), lambda b, h: (b, k_off + h)),
                  pl.BlockSpec((seqlen, HEAD_DIM), lambda b, h: (b, h))],
        out_specs=pl.BlockSpec((seqlen, gw), lambda b, h: (b, h)),
        compiler_params=_params("arbitrary", "arbitrary"),
        name="attention",
    )(qk, qk, v)


def kernel(x, ffn_norm, ffn_w_gate, ffn_w_up, ffn_w_down, mix_norm, hyb_in_proj,
           ssd_conv_w, ssd_conv_b, ssd_A_log, ssd_dt_bias, ssd_D, ssd_gnorm,
           hyb_out_proj, attn_w_qkv, attn_q_norm, attn_k_norm, attn_w_o, final_norm):
    b, l, d = x.shape
    t = b * l
    x = x.reshape(t, d)

    wg_all = ffn_w_gate.astype(BF16)
    wu_all = ffn_w_up.astype(BF16)
    wd_all = ffn_w_down

    def ffn_half(x, i, h):
        hn = _rmsnorm(x, ffn_norm[i, h])
        return _ffn(hn, x, wg_all, wu_all, wd_all, i, h)

    x = ffn_half(x, 0, 0)
    hn = _rmsnorm(x, mix_norm[0])
    w_in = hyb_in_proj[0]
    d_ssd = ssd_gnorm.shape[1]
    d_fourier = 2 * d - d_ssd
    n_bc = 2 * SSD_GROUPS * D_STATE
    o1 = d_fourier
    o2 = o1 + d_ssd
    o3 = o2 + d_ssd
    o4 = o3 + n_bc
    n_dt = w_in.shape[1] - o4
    w_dt = jnp.pad(w_in[:, o4:], ((0, 0), (0, 2 * LANES - n_dt)))
    w_in_bf = w_in.astype(BF16)
    u = _matmul(hn, w_in_bf, BF16, cols=(0, o1))
    z = _matmul(hn, w_in_bf, BF16, cols=(o1, o2))
    xs_raw = _matmul(hn, w_in_bf, F32, cols=(o2, o3))
    bc_raw = _matmul(hn, w_in_bf, F32, cols=(o3, o4))
    dt_raw = _matmul(hn, w_dt.astype(BF16), F32)[:, :n_dt]
    y_f = _fourier_mix(u, b, l)
    y_s = _ssd_mixer(xs_raw, bc_raw, z, dt_raw, ssd_conv_w[0], ssd_conv_b[0], ssd_A_log[0],
                     ssd_dt_bias[0], ssd_D[0], ssd_gnorm[0], b, l)
    x = _matmul2_res(y_f, y_s, hyb_out_proj[0].astype(BF16), x)
    x = ffn_half(x, 0, 1)

    x = ffn_half(x, 1, 0)
    hn = _rmsnorm(x, mix_norm[1])
    n_qk = (N_HEADS + N_KV_HEADS) * HEAD_DIM
    qk = _qk_proj(hn, _pair_layout(attn_w_qkv[0][:, :n_qk]).astype(BF16), attn_q_norm[0], attn_k_norm[0], l)
    v = _matmul(hn, attn_w_qkv[0][:, n_qk:].astype(BF16), BF16)
    o = _attention(qk, v, b, l)
    x = _matmul_res(o, attn_w_o[0].astype(BF16), x)
    x = ffn_half(x, 1, 1)

    return _rmsnorm(x, final_norm, F32).reshape(b, l, d)
```

```python
import functools
import math

import jax
import jax.numpy as jnp
from jax import lax
from jax.experimental import pallas as pl
from jax.experimental.pallas import tpu as pltpu

EPS = 1e-6
GRID_W = 64
FOURIER_GROUPS = 8
SSD_HEAD_DIM = 64
SSD_GROUPS = 8
D_STATE = 128
D_CONV = 5
SSD_CHUNK = 128
N_HEADS = 32
N_KV_HEADS = 8
KV_GROUP = N_HEADS // N_KV_HEADS
HEAD_DIM = 128
ROPE_HALF = HEAD_DIM // 2
ROPE_THETA = 10000.0

LANES = 128
SUBLANES = 8
V7X_VMEM_LIMIT_BYTES = 56 * 1024 * 1024

BF16 = jnp.bfloat16
F32 = jnp.float32
LOG2_E = math.log2(math.e)


def _params(*semantics):
    return pltpu.CompilerParams(dimension_semantics=semantics,
                                vmem_limit_bytes=V7X_VMEM_LIMIT_BYTES)


def _dot(a, b):
    return jnp.dot(a, b, preferred_element_type=F32)


def _dot_nt(a, b):
    return lax.dot_general(a, b, (((1,), (1,)), ((), ())), preferred_element_type=F32)


def _dot_tn(a, b):
    return lax.dot_general(a, b, (((0,), (0,)), ((), ())), preferred_element_type=F32)


def _sigmoid(x):
    return 1.0 / (1.0 + jnp.exp(-x))


def _rmsnorm_kernel(x_ref, w_ref, o_ref):
    x = x_ref[...]
    ms = jnp.mean(x * x, axis=-1, keepdims=True)
    o_ref[...] = (x * lax.rsqrt(ms + EPS) * w_ref[...]).astype(o_ref.dtype)


def _rmsnorm(x, w, out_dtype=BF16, *, tm=512):
    t, d = x.shape
    return pl.pallas_call(
        _rmsnorm_kernel,
        out_shape=jax.ShapeDtypeStruct((t, d), out_dtype),
        grid=(t // tm,),
        in_specs=[pl.BlockSpec((tm, d), lambda i: (i, 0)),
                  pl.BlockSpec((1, d), lambda i: (0, 0))],
        out_specs=pl.BlockSpec((tm, d), lambda i: (i, 0)),
        compiler_params=_params("arbitrary"),
        name="rmsnorm",
    )(x, w.reshape(1, d))


FFN_DOWN_CHUNK = 256
FFN_TILE_M = 1024
FFN_TILE_F = 256
FFN_VMEM_LIMIT_BYTES = 60 * 1024 * 1024


def _ffn_kernel(hn_ref, x_hbm, wg_ref, wu_ref, wd_ref, o_ref, x_sem):
    i = pl.program_id(0)
    j = pl.program_id(1)
    tm = o_ref.shape[0]

    def x_copy():
        return pltpu.make_async_copy(x_hbm.at[pl.ds(i * tm, tm), :], o_ref, x_sem)

    @pl.when(j == 0)
    def _():
        x_copy().start()

    hn = hn_ref[...]
    g = _dot(hn, wg_ref[...])
    u = _dot(hn, wu_ref[...])
    a = (0.5 * g / (1.0 + jnp.exp(-g)) * u).astype(BF16)

    @pl.when(j == 0)
    def _():
        x_copy().wait()

    for n in range(0, o_ref.shape[1], FFN_DOWN_CHUNK):
        cols = slice(n, n + FFN_DOWN_CHUNK)
        o_ref[:, cols] += _dot(a, wd_ref[:, cols].astype(BF16))


def _ffn(hn, x, wg, wu, wd, layer, half, *, tm=FFN_TILE_M, tf=FFN_TILE_F):
    t, d = x.shape
    f = wg.shape[-1]
    tm = min(tm, t)
    assert f % tf == 0, (f, tf)
    return pl.pallas_call(
        _ffn_kernel,
        out_shape=jax.ShapeDtypeStruct((t, d), F32),
        grid=(t // tm, f // tf),
        in_specs=[pl.BlockSpec((tm, d), lambda i, j: (i, 0), pipeline_mode=pl.Buffered(1)),
                  pl.BlockSpec(memory_space=pl.ANY),
                  pl.BlockSpec((None, None, d, tf), lambda i, j: (layer, half, 0, j)),
                  pl.BlockSpec((None, None, d, tf), lambda i, j: (layer, half, 0, j)),
                  pl.BlockSpec((None, None, tf, d), lambda i, j: (layer, half, j, 0))],
        out_specs=pl.BlockSpec((tm, d), lambda i, j: (i, 0)),
        scratch_shapes=[pltpu.SemaphoreType.DMA(())],
        compiler_params=pltpu.CompilerParams(dimension_semantics=("arbitrary", "arbitrary"),
                                             vmem_limit_bytes=FFN_VMEM_LIMIT_BYTES),
        name="ffn",
    )(hn, x, wg, wu, wd)


def _mm_kernel(a_ref, w_ref, o_ref):
    o_ref[...] = _dot(a_ref[...], w_ref[...]).astype(o_ref.dtype)


def _matmul(a, w, out_dtype, *, cols=None, tm=1024, tn=1024):
    t, k = a.shape
    c0, c1 = cols if cols is not None else (0, w.shape[1])
    n = c1 - c0
    tm, tn = min(tm, t), min(tn, n)
    assert n % tn == 0 and c0 % tn == 0, (c0, c1, tn)
    j0 = c0 // tn
    return pl.pallas_call(
        _mm_kernel,
        out_shape=jax.ShapeDtypeStruct((t, n), out_dtype),
        grid=(t // tm, n // tn),
        in_specs=[pl.BlockSpec((tm, k), lambda i, j: (i, 0)),
                  pl.BlockSpec((k, tn), lambda i, j: (0, j0 + j))],
        out_specs=pl.BlockSpec((tm, tn), lambda i, j: (i, j)),
        compiler_params=_params("arbitrary", "arbitrary"),
        name="matmul",
    )(a, w)


def _mm_res_kernel(a_ref, w_ref, x_ref, o_ref):
    o_ref[...] = x_ref[...] + _dot(a_ref[...], w_ref[...])


def _matmul_res(a, w, x, *, tm=1024, tn=1024):
    t, k = a.shape
    n = w.shape[1]
    tm, tn = min(tm, t), min(tn, n)
    return pl.pallas_call(
        _mm_res_kernel,
        out_shape=jax.ShapeDtypeStruct((t, n), F32),
        grid=(t // tm, n // tn),
        in_specs=[pl.BlockSpec((tm, k), lambda i, j: (i, 0)),
                  pl.BlockSpec((k, tn), lambda i, j: (0, j)),
                  pl.BlockSpec((tm, tn), lambda i, j: (i, j))],
        out_specs=pl.BlockSpec((tm, tn), lambda i, j: (i, j)),
        compiler_params=_params("arbitrary", "arbitrary"),
        name="matmul_res",
    )(a, w, x)


def _mm2_res_kernel(a0_ref, a1_ref, w_ref, x_ref, o_ref):
    k = pl.program_id(2)

    @pl.when(k == 0)
    def _():
        o_ref[...] = x_ref[...] + _dot(a0_ref[...], w_ref[...])

    @pl.when(k > 0)
    def _():
        o_ref[...] += _dot(a1_ref[...], w_ref[...])


def _matmul2_res(a0, a1, w, x, *, tm=1024, tn=1024):
    t, tk = a0.shape
    n = w.shape[1]
    tm, tn = min(tm, t), min(tn, n)
    nk = 1 + a1.shape[1] // tk
    return pl.pallas_call(
        _mm2_res_kernel,
        out_shape=jax.ShapeDtypeStruct((t, n), F32),
        grid=(t // tm, n // tn, nk),
        in_specs=[pl.BlockSpec((tm, tk), lambda i, j, k: (i, 0)),
                  pl.BlockSpec((tm, tk), lambda i, j, k: (i, jnp.maximum(k - 1, 0))),
                  pl.BlockSpec((tk, tn), lambda i, j, k: (k, j)),
                  pl.BlockSpec((tm, tn), lambda i, j, k: (i, j))],
        out_specs=pl.BlockSpec((tm, tn), lambda i, j, k: (i, j)),
        compiler_params=_params("arbitrary", "arbitrary", "arbitrary"),
        name="matmul2_res",
    )(a0, a1, w, x)


def _fourier_kernel(u_ref, wc_ref, dl_ref, o_ref, *, scale):
    c = u_ref.shape[1]
    w1 = _dot(u_ref[...], wc_ref[...]).astype(BF16)
    stacked = jnp.concatenate([w1[:, :c], w1[:, c:]], axis=0)
    o_ref[...] = (_dot(dl_ref[...], stacked) * scale).astype(o_ref.dtype)


def _dft_tables(n):
    idx = jnp.arange(n, dtype=jnp.int32)
    ang = ((idx[:, None] * idx[None, :]) % n).astype(F32) * (2.0 * math.pi / n)
    return jnp.cos(ang), jnp.sin(ang)


def _fourier_mix(u, batch, seqlen):
    t, width = u.shape
    c = width // FOURIER_GROUPS
    cc, sc = _dft_tables(c)
    cl, sl = _dft_tables(seqlen)
    wc = jnp.concatenate([cc, sc], axis=1).astype(BF16)
    dl = jnp.concatenate([cl, -sl], axis=1).astype(BF16)
    return pl.pallas_call(
        functools.partial(_fourier_kernel, scale=1.0 / math.sqrt(seqlen * c)),
        out_shape=jax.ShapeDtypeStruct((t, width), BF16),
        grid=(batch, FOURIER_GROUPS),
        in_specs=[pl.BlockSpec((seqlen, c), lambda b, g: (b, g)),
                  pl.BlockSpec((c, 2 * c), lambda b, g: (0, 0)),
                  pl.BlockSpec((seqlen, 2 * seqlen), lambda b, g: (0, 0))],
        out_specs=pl.BlockSpec((seqlen, c), lambda b, g: (b, g)),
        compiler_params=_params("arbitrary", "arbitrary"),
        name="fourier",
    )(u, wc, dl)


def _split3(x):
    h1 = x.astype(BF16)
    r1 = x - h1.astype(F32)
    h2 = r1.astype(BF16)
    h3 = (r1 - h2.astype(F32)).astype(BF16)
    return h1, h2, h3


def _softplus(v):
    return jnp.maximum(v, 0.0) + jnp.log(1.0 + jnp.exp(-jnp.abs(v)))


def _conv_silu_chunk(raw_ref, w_ref, b_ref, win_ref, c, n_chunks):
    q = SSD_CHUNK
    seqlen = raw_ref.shape[0]
    r0 = pl.multiple_of(c * q, q)
    top = raw_ref[pl.ds(pl.multiple_of(jnp.maximum(r0 - SUBLANES, 0), SUBLANES), SUBLANES), :]
    bot = raw_ref[pl.ds(pl.multiple_of(jnp.minimum(r0 + q, seqlen - SUBLANES), SUBLANES), SUBLANES), :]
    win_ref[0:SUBLANES, :] = jnp.where(c > 0, top, 0.0)
    win_ref[SUBLANES:SUBLANES + q, :] = raw_ref[pl.ds(r0, q), :]
    win_ref[SUBLANES + q:2 * SUBLANES + q, :] = jnp.where(c < n_chunks - 1, bot, 0.0)
    acc = b_ref[...]
    pad = (D_CONV - 1) // 2
    for j in range(D_CONV):
        lo = SUBLANES - pad + j
        acc = acc + w_ref[j:j + 1, :] * win_ref[lo:lo + q, :]
    return acc * _sigmoid(acc)


def _dt_tables(dtc_ref, dtr_ref, biasc_ref, biasr_ref, alogc_ref, alogr_ref,
               cum_ref, ecum_ref, dec_ref, rrow_ref, wrow_ref, lo_tri, up_tri, nh):
    q = SSD_CHUNK
    n_chunks = cum_ref.shape[0]
    per = 2 * nh
    dt = _softplus(dtc_ref[...] + biasc_ref[...])
    parts = _split3(dt * (-jnp.exp(alogc_ref[...])))
    cum_f = sum(_dot(lo_tri, h) for h in parts)
    cum_b = sum(_dot(up_tri, h) for h in parts)
    fwd_col = (lax.broadcasted_iota(jnp.int32, cum_f.shape, 1) % per) < nh
    cum = jnp.where(fwd_col, cum_f, cum_b)
    ecum = jnp.exp(cum)
    dec = jnp.exp(jnp.where(fwd_col[:1], cum[q - 1:q, :], cum[0:1, :]))
    dt_r = _softplus(dtr_ref[...] + biasr_ref[...])
    parts_r = _split3(dt_r * (-jnp.exp(alogr_ref[...])))
    cum_fr = sum(_dot(h, up_tri) for h in parts_r)
    cum_br = sum(_dot(h, lo_tri) for h in parts_r)
    fwd_row = (lax.broadcasted_iota(jnp.int32, cum_fr.shape, 0) % per) < nh
    cum_r = jnp.where(fwd_row, cum_fr, cum_br)
    rrow = cum_r - jnp.log(dt_r)
    total_r = jnp.where(fwd_row[:, :1], cum_r[:, q - 1:q], cum_r[:, 0:1])
    wrow = jnp.exp(total_r - cum_r) * dt_r
    cum = cum * LOG2_E
    rrow = rrow * LOG2_E
    for c in range(n_chunks):
        cols = slice(c * per, (c + 1) * per)
        cum_ref[c] = cum[:, cols]
        ecum_ref[c] = ecum[:, cols]
        dec_ref[c] = dec[:, cols]
        rrow_ref[c] = rrow[cols, :]
        wrow_ref[c] = wrow[cols, :]


def _ssd_kernel(xs_ref, b_ref, c_ref, z_ref, dtc_ref, dtr_ref, biasc_ref, biasr_ref,
                alogc_ref, alogr_ref, cwx_ref, cwb_ref, cwc_ref, cbx_ref, cbb_ref, cbc_ref,
                d_ref, gn_ref, o_ref, xc_ref, bt_ref, cv_ref, y_ref, sf_ref, sb_ref,
                cum_ref, ecum_ref, dec_ref, rrow_ref, wrow_ref, winx_ref, winb_ref, winc_ref):
    q = SSD_CHUNK
    seqlen, width = xs_ref.shape
    n_chunks = seqlen // q
    nh = width // SSD_HEAD_DIM
    n_pairs = nh // 2
    half = SSD_HEAD_DIM

    t_idx = lax.broadcasted_iota(jnp.int32, (q, q), 0)
    s_idx = lax.broadcasted_iota(jnp.int32, (q, q), 1)
    lower = s_idx <= t_idx
    upper = s_idx >= t_idx
    lo_tri = jnp.where(lower, 1.0, 0.0).astype(BF16)
    up_tri = jnp.where(upper, 1.0, 0.0).astype(BF16)
    lane = lax.broadcasted_iota(jnp.int32, (q, LANES), 1)
    lane_lo = lane < half
    lane_row_lo = lax.broadcasted_iota(jnp.int32, (1, LANES), 1) < half
    srow = lax.broadcasted_iota(jnp.int32, (2 * D_STATE, LANES), 0)
    slane = lax.broadcasted_iota(jnp.int32, (2 * D_STATE, LANES), 1)
    diag_blocks = (srow < D_STATE) == (slane < half)

    sf_ref[...] = jnp.zeros_like(sf_ref)
    sb_ref[...] = jnp.zeros_like(sb_ref)
    _dt_tables(dtc_ref, dtr_ref, biasc_ref, biasr_ref, alogc_ref, alogr_ref,
               cum_ref, ecum_ref, dec_ref, rrow_ref, wrow_ref, lo_tri, up_tri, nh)

    def carry_update(s_ref, p, bwt_pair, xp_bf, decay_row):
        prod = _dot(bwt_pair, xp_bf)
        s_ref[p] = s_ref[p] * decay_row + jnp.where(diag_blocks, prod, 0.0)

    def forward_chunk(c, _):
        r0 = pl.multiple_of(c * q, q)
        rows = pl.ds(r0, q)
        xq = _conv_silu_chunk(xs_ref, cwx_ref, cbx_ref, winx_ref, c, n_chunks)
        bq = _conv_silu_chunk(b_ref, cwb_ref, cbb_ref, winb_ref, c, n_chunks)
        cq = _conv_silu_chunk(c_ref, cwc_ref, cbc_ref, winc_ref, c, n_chunks)
        bt = bq.T
        xc_ref[rows, :] = xq.astype(BF16)
        bt_ref[c] = bt
        cv_ref[rows, :] = cq
        cum = cum_ref[c]
        ecum = ecum_ref[c]
        dec = dec_ref[c]
        r_row = rrow_ref[c]
        w_row = wrow_ref[c]
        cb = _dot_nt(cq.astype(BF16), bq.astype(BF16))
        for p in range(n_pairs):
            ms, ces, bws, decays = [], [], [], []
            for k in (2 * p, 2 * p + 1):
                e_f = jnp.where(lower, cum[:, k:k + 1] - r_row[k:k + 1, :], -jnp.inf)
                e_b = jnp.where(upper, cum[:, nh + k:nh + k + 1] - r_row[nh + k:nh + k + 1, :], -jnp.inf)
                ms.append((cb * (jnp.exp2(e_f) + jnp.exp2(e_b))).astype(BF16))
                ces.append((cq * ecum[:, k:k + 1]).astype(BF16))
                bws.append((bt * w_row[k:k + 1, :]).astype(BF16))
                decays.append(dec[:, k:k + 1])
            cols = slice(p * LANES, (p + 1) * LANES)
            xp = xq[:, cols]
            xp_bf = xp.astype(BF16)
            zero = jnp.zeros_like(xp_bf)
            lhs = jnp.concatenate(ms + ces, axis=1)
            rhs = jnp.concatenate([jnp.where(lane_lo, xp_bf, zero), jnp.where(lane_lo, zero, xp_bf),
                                   sf_ref[p].astype(BF16)], axis=0)
            y_ref[rows, cols] = _dot(lhs, rhs)
            carry_update(sf_ref, p, jnp.concatenate(bws, axis=0), xp_bf,
                         jnp.where(lane_row_lo, decays[0], decays[1]))
        return 0

    def backward_chunk(i, _):
        c = n_chunks - 1 - i
        r0 = pl.multiple_of(c * q, q)
        rows = pl.ds(r0, q)
        bt = bt_ref[c]
        cq = cv_ref[rows, :]
        ecum = ecum_ref[c]
        dec = dec_ref[c]
        w_row = wrow_ref[c]
        ssq = jnp.zeros((q, 1), F32)
        for p in range(n_pairs):
            ces, bws, decays = [], [], []
            for k in (2 * p + nh, 2 * p + 1 + nh):
                ces.append((cq * ecum[:, k:k + 1]).astype(BF16))
                bws.append((bt * w_row[k:k + 1, :]).astype(BF16))
                decays.append(dec[:, k:k + 1])
            cols = slice(p * LANES, (p + 1) * LANES)
            xp_bf = xc_ref[rows, cols]
            y = y_ref[rows, cols] + _dot(jnp.concatenate(ces, axis=1), sb_ref[p].astype(BF16))
            carry_update(sb_ref, p, jnp.concatenate(bws, axis=0), xp_bf,
                         jnp.where(lane_row_lo, decays[0], decays[1]))
            zz = z_ref[rows, cols].astype(F32)
            y = (y + d_ref[:, cols] * xp_bf.astype(F32)) * (zz * _sigmoid(zz))
            y_ref[rows, cols] = y
            ssq = ssq + jnp.sum(y * y, axis=-1, keepdims=True)
        inv = lax.rsqrt(ssq * (1.0 / width) + EPS)
        for p in range(n_pairs):
            cols = slice(p * LANES, (p + 1) * LANES)
            o_ref[rows, cols] = (y_ref[rows, cols] * inv * gn_ref[:, cols]).astype(o_ref.dtype)
        return 0

    lax.fori_loop(0, n_chunks, forward_chunk, 0)
    lax.fori_loop(0, n_chunks, backward_chunk, 0)


def _ssd_mixer(xs_raw, bc_raw, z, dt_raw, conv_w, conv_b, a_log, dt_bias, d_skip, gnorm_w,
               batch, seqlen):
    t, d_ssd = xs_raw.shape
    g = SSD_GROUPS
    gw = d_ssd // g
    kh = gw // SSD_HEAD_DIM
    n = D_STATE
    q = SSD_CHUNK
    nc = seqlen // q
    per = 2 * kh
    dt6 = dt_raw.reshape(batch, nc, q, 2, g, kh)
    dtc = dt6.transpose(0, 4, 2, 1, 3, 5).reshape(batch, g, q, nc * per)
    dtr = dtc.transpose(0, 1, 3, 2)
    per_group = lambda v: jnp.tile(v.reshape(2, g, kh).transpose(1, 0, 2).reshape(g, per), (1, nc))
    bias_c = per_group(dt_bias)[:, None, :]
    bias_r = per_group(dt_bias)[:, :, None]
    alog_c = per_group(a_log)[:, None, :]
    alog_r = per_group(a_log)[:, :, None]
    cw_x, cw_bc = conv_w[:, :d_ssd], conv_w[:, d_ssd:]
    cb_x, cb_bc = conv_b[None, :d_ssd], conv_b[None, d_ssd:]
    d_full = jnp.repeat(d_skip, SSD_HEAD_DIM)[None, :]
    slab = lambda w: pl.BlockSpec((seqlen, w), lambda b, gi: (b, gi))
    return pl.pallas_call(
        _ssd_kernel,
        out_shape=jax.ShapeDtypeStruct((t, d_ssd), BF16),
        grid=(batch, g),
        in_specs=[slab(gw),
                  slab(n),
                  pl.BlockSpec((seqlen, n), lambda b, gi: (b, g + gi)),
                  slab(gw),
                  pl.BlockSpec((None, None, q, nc * per), lambda b, gi: (b, gi, 0, 0)),
                  pl.BlockSpec((None, None, nc * per, q), lambda b, gi: (b, gi, 0, 0)),
                  pl.BlockSpec((None, 1, nc * per), lambda b, gi: (gi, 0, 0)),
                  pl.BlockSpec((None, nc * per, 1), lambda b, gi: (gi, 0, 0)),
                  pl.BlockSpec((None, 1, nc * per), lambda b, gi: (gi, 0, 0)),
                  pl.BlockSpec((None, nc * per, 1), lambda b, gi: (gi, 0, 0)),
                  pl.BlockSpec((D_CONV, gw), lambda b, gi: (0, gi)),
                  pl.BlockSpec((D_CONV, n), lambda b, gi: (0, gi)),
                  pl.BlockSpec((D_CONV, n), lambda b, gi: (0, g + gi)),
                  pl.BlockSpec((1, gw), lambda b, gi: (0, gi)),
                  pl.BlockSpec((1, n), lambda b, gi: (0, gi)),
                  pl.BlockSpec((1, n), lambda b, gi: (0, g + gi)),
                  pl.BlockSpec((1, gw), lambda b, gi: (0, gi)),
                  pl.BlockSpec((1, gw), lambda b, gi: (0, gi))],
        out_specs=slab(gw),
        scratch_shapes=[pltpu.VMEM((seqlen, gw), BF16),
                        pltpu.VMEM((nc, n, q), F32),
                        pltpu.VMEM((seqlen, n), F32),
                        pltpu.VMEM((seqlen, gw), F32),
                        pltpu.VMEM((kh // 2, 2 * n, LANES), F32),
                        pltpu.VMEM((kh // 2, 2 * n, LANES), F32),
                        pltpu.VMEM((nc, q, per), F32),
                        pltpu.VMEM((nc, q, per), F32),
                        pltpu.VMEM((nc, 1, per), F32),
                        pltpu.VMEM((nc, per, q), F32),
                        pltpu.VMEM((nc, per, q), F32),
                        pltpu.VMEM((q + 2 * SUBLANES, gw), F32),
                        pltpu.VMEM((q + 2 * SUBLANES, n), F32),
                        pltpu.VMEM((q + 2 * SUBLANES, n), F32)],
        compiler_params=_params("arbitrary", "arbitrary"),
        name="ssd",
    )(xs_raw, bc_raw, bc_raw, z, dtc, dtr, bias_c, bias_r, alog_c, alog_r,
      cw_x, cw_bc, cw_bc, cb_x, cb_bc, cb_bc, d_full, gnorm_w[None, :])


def _pair_layout(v):
    lead = v.shape[:-1]
    v = v.reshape(*lead, -1, 2, 2, ROPE_HALF // 2)
    return jnp.swapaxes(v, -3, -2).reshape(*lead, -1)


def _rope_tables(seqlen):
    pos = jnp.arange(seqlen)
    inv_freq = ROPE_THETA ** (-jnp.arange(0, ROPE_HALF, 2, dtype=F32) / ROPE_HALF)
    ang_row = (pos // GRID_W).astype(F32)[:, None] * inv_freq
    ang_col = (pos % GRID_W).astype(F32)[:, None] * inv_freq
    cos = jnp.concatenate([jnp.cos(ang_row), jnp.cos(ang_col)] * 2, axis=1)
    sin = jnp.concatenate([-jnp.sin(ang_row), -jnp.sin(ang_col),
                           jnp.sin(ang_row), jnp.sin(ang_col)], axis=1)
    return cos, sin


def _qk_kernel(a_ref, w_ref, nw_ref, sc_ref, cos_ref, sin_ref, o_ref):
    acc = _dot(a_ref[...], w_ref[...])

    @pl.when(pl.program_id(1) >= 0)
    def _():
        cos = cos_ref[...]
        sin = sin_ref[...]
        for h in range(acc.shape[1] // HEAD_DIM):
            cols = slice(h * HEAD_DIM, (h + 1) * HEAD_DIM)
            xh = acc[:, cols]
            ms = jnp.mean(xh * xh, axis=-1, keepdims=True)
            y = xh * lax.rsqrt(ms + EPS) * nw_ref[:, cols]
            partner = pltpu.roll(y, HEAD_DIM // 2, 1)
            o_ref[:, cols] = ((y * cos + partner * sin) * sc_ref[:, cols]).astype(o_ref.dtype)


def _qk_proj(hn, w, q_norm, k_norm, seqlen, *, tm=1024, tn=512):
    t, k = hn.shape
    n = w.shape[1]
    tm = min(tm, seqlen)
    nq = N_HEADS * HEAD_DIM
    norm_row = jnp.concatenate([jnp.tile(_pair_layout(q_norm), N_HEADS),
                                jnp.tile(_pair_layout(k_norm), N_KV_HEADS)])[None, :]
    scale_row = jnp.concatenate([jnp.full((nq,), HEAD_DIM ** -0.5 * math.log2(math.e), F32),
                                 jnp.ones((n - nq,), F32)])[None, :]
    cos, sin = _rope_tables(seqlen)
    row_tiles = seqlen // tm
    return pl.pallas_call(
        _qk_kernel,
        out_shape=jax.ShapeDtypeStruct((t, n), BF16),
        grid=(t // tm, n // tn),
        in_specs=[pl.BlockSpec((tm, k), lambda i, j: (i, 0)),
                  pl.BlockSpec((k, tn), lambda i, j: (0, j)),
                  pl.BlockSpec((1, tn), lambda i, j: (0, j)),
                  pl.BlockSpec((1, tn), lambda i, j: (0, j)),
                  pl.BlockSpec((tm, HEAD_DIM), lambda i, j: (i % row_tiles, 0)),
                  pl.BlockSpec((tm, HEAD_DIM), lambda i, j: (i % row_tiles, 0))],
        out_specs=pl.BlockSpec((tm, tn), lambda i, j: (i, j)),
        compiler_params=_params("arbitrary", "arbitrary"),
        name="qk_proj",
    )(hn, w, norm_row, scale_row, cos, sin)


def _attn_kernel(q_ref, k_ref, v_ref, o_ref, *, tq):
    seqlen = k_ref.shape[0]
    k = k_ref[...]
    v = v_ref[...]

    def q_block(c, _):
        rows = pl.ds(pl.multiple_of(c * tq, tq), tq)
        for g in range(KV_GROUP):
            cols = slice(g * HEAD_DIM, (g + 1) * HEAD_DIM)
            s = _dot_nt(q_ref[rows, cols], k)
            p = jnp.exp2(s - jnp.max(s, axis=-1, keepdims=True))
            denom = jnp.sum(p, axis=-1, keepdims=True)
            o_ref[rows, cols] = (_dot(p.astype(BF16), v) / denom).astype(o_ref.dtype)
        return 0

    lax.fori_loop(0, seqlen // tq, q_block, 0, unroll=2)


def _attention(qk, v, batch, seqlen, *, tq=512):
    t = qk.shape[0]
    gw = KV_GROUP * HEAD_DIM
    k_off = N_HEADS
    return pl.pallas_call(
        functools.partial(_attn_kernel, tq=min(tq, seqlen)),
        out_shape=jax.ShapeDtypeStruct((t, N_HEADS * HEAD_DIM), BF16),
        grid=(batch, N_KV_HEADS),
        in_specs=[pl.BlockSpec((seqlen, gw), lambda b, h: (b, h)),
                  pl.BlockSpec((seqlen, HEAD_DIM), lambda b, h: (b, k_off + h)),
                  pl.BlockSpec((seqlen, HEAD_DIM), lambda b, h: (b, h))],
        out_specs=pl.BlockSpec((seqlen, gw), lambda b, h: (b, h)),
        compiler_params=_params("arbitrary", "arbitrary"),
        name="attention",
    )(qk, qk, v)


def kernel(x, ffn_norm, ffn_w_gate, ffn_w_up, ffn_w_down, mix_norm, hyb_in_proj,
           ssd_conv_w, ssd_conv_b, ssd_A_log, ssd_dt_bias, ssd_D, ssd_gnorm,
           hyb_out_proj, attn_w_qkv, attn_q_norm, attn_k_norm, attn_w_o, final_norm):
    b, l, d = x.shape
    t = b * l
    x = x.reshape(t, d)

    wg_all = ffn_w_gate.astype(BF16)
    wu_all = ffn_w_up.astype(BF16)
    wd_all = ffn_w_down

    def ffn_half(x, i, h):
        hn = _rmsnorm(x, ffn_norm[i, h])
        return _ffn(hn, x, wg_all, wu_all, wd_all, i, h)

    x = ffn_half(x, 0, 0)
    hn = _rmsnorm(x, mix_norm[0])
    w_in = hyb_in_proj[0]
    d_ssd = ssd_gnorm.shape[1]
    d_fourier = 2 * d - d_ssd
    n_bc = 2 * SSD_GROUPS * D_STATE
    o1 = d_fourier
    o2 = o1 + d_ssd
    o3 = o2 + d_ssd
    o4 = o3 + n_bc
    n_dt = w_in.shape[1] - o4
    w_dt = jnp.pad(w_in[:, o4:], ((0, 0), (0, 2 * LANES - n_dt)))
    w_in_bf = w_in.astype(BF16)
    u = _matmul(hn, w_in_bf, BF16, cols=(0, o1))
    z = _matmul(hn, w_in_bf, BF16, cols=(o1, o2))
    xs_raw = _matmul(hn, w_in_bf, F32, cols=(o2, o3))
    bc_raw = _matmul(hn, w_in_bf, F32, cols=(o3, o4))
    dt_raw = _matmul(hn, w_dt.astype(BF16), F32)[:, :n_dt]
    y_f = _fourier_mix(u, b, l)
    y_s = _ssd_mixer(xs_raw, bc_raw, z, dt_raw, ssd_conv_w[0], ssd_conv_b[0], ssd_A_log[0],
                     ssd_dt_bias[0], ssd_D[0], ssd_gnorm[0], b, l)
    x = _matmul2_res(y_f, y_s, hyb_out_proj[0].astype(BF16), x)
    x = ffn_half(x, 0, 1)

    x = ffn_half(x, 1, 0)
    hn = _rmsnorm(x, mix_norm[1])
    n_qk = (N_HEADS + N_KV_HEADS) * HEAD_DIM
    qk = _qk_proj(hn, _pair_layout(attn_w_qkv[0][:, :n_qk]).astype(BF16), attn_q_norm[0], attn_k_norm[0], l)
    v = _matmul(hn, attn_w_qkv[0][:, n_qk:].astype(BF16), BF16)
    o = _attention(qk, v, b, l)
    x = _matmul_res(o, attn_w_o[0].astype(BF16), x)
    x = ffn_half(x, 1, 1)

    return _rmsnorm(x, final_norm, F32).reshape(b, l, d)
```
